```python
import math
import jax, jax.numpy as jnp
from jax import lax
import numpy as np

D_MODEL = 1024
BATCH = 32
SEQ = 2048
DEPTH = 1

PLE_DIM = 256
N_DIFF_HEADS = 8
DIFF_HEAD_DIM = 64
DIFF_V_DIM = 2 * DIFF_HEAD_DIM
N_RET_HEADS = 4
RET_KEY_DIM = 128
RET_VAL_DIM = 2 * RET_KEY_DIM
D_FF = 4 * D_MODEL
Q_BLOCK = 128
RET_CHUNK = 128
EPS = 1e-6

DIFF_QK_W = N_DIFF_HEADS * 2 * DIFF_HEAD_DIM
DIFF_V_W = N_DIFF_HEADS * DIFF_V_DIM
RET_QK_W = N_RET_HEADS * RET_KEY_DIM
RET_V_W = N_RET_HEADS * RET_VAL_DIM
IN_SPLITS = (DIFF_QK_W, DIFF_QK_W, DIFF_V_W,
             RET_QK_W, RET_QK_W, RET_V_W, RET_V_W,
             D_MODEL, D_MODEL)
D_IN = sum(IN_SPLITS)

kernel_name = "hybrid_diffattn_retention_gated_block"


def rms_norm(x, g):
    xf = x.astype(jnp.float32)
    y = xf * lax.rsqrt(jnp.mean(xf * xf, axis=-1, keepdims=True) + EPS)
    return (y * g.astype(jnp.float32)).astype(x.dtype)


def alibi_slopes(n_heads):
    return 2.0 ** (-8.0 * jnp.arange(1, n_heads + 1, dtype=jnp.float32) / n_heads)


def diff_attention(q, k, v, lam):
    B, S, H, _, d = q.shape
    e = v.shape[-1]
    n_blocks = S // Q_BLOCK
    scale = d ** -0.5
    kpos = jnp.arange(S)
    slopes = alibi_slopes(H)[:, None, None, None]

    def block(i):
        start = i * Q_BLOCK
        qb = lax.dynamic_slice_in_dim(q, start, Q_BLOCK, axis=1)
        s = jnp.einsum('bqhcd,bkhcd->bhcqk', qb, k).astype(jnp.float32) * scale
        qpos = start + jnp.arange(Q_BLOCK)
        dist = (qpos[:, None] - kpos[None, :]).astype(jnp.float32)
        logits = jnp.where(dist >= 0, s - slopes * dist, -jnp.inf)
        probs = jax.nn.softmax(logits, axis=-1)
        a = probs[:, :, 0] - lam * probs[:, :, 1]
        return jnp.einsum('bhqk,bkhe->bqhe', a.astype(v.dtype), v)

    out = lax.map(block, jnp.arange(n_blocks))
    return out.transpose(1, 0, 2, 3, 4).reshape(B, S, H, e)


def retention(q, k, v):
    B, S, H, dk = q.shape
    dv = v.shape[-1]
    C = RET_CHUNK
    n_chunks = S // C
    dt = q.dtype
    log_gamma = jnp.log(1.0 - 2.0 ** (-5.0 - jnp.arange(H, dtype=jnp.float32)))
    pos = jnp.arange(C, dtype=jnp.float32)
    rel = pos[:, None] - pos[None, :]
    decay_intra = jnp.where(rel >= 0, jnp.exp(log_gamma[:, None, None] * rel), 0.0).astype(dt)
    q_decay = jnp.exp(log_gamma[None, :] * (pos[:, None] + 1.0)).astype(dt)
    k_decay = jnp.exp(log_gamma[:, None] * (C - 1.0 - pos[None, :])).astype(dt)
    chunk_decay = jnp.exp(log_gamma * C).astype(dt)

    k = k * (dk ** -0.5)
    to_chunks = lambda t: t.reshape(B, n_chunks, C, H, t.shape[-1]).transpose(1, 0, 2, 3, 4)
    qc, kc, vc = to_chunks(q), to_chunks(k), to_chunks(v)

    def step(R, inp):
        qi, ki, vi = inp
        s = jnp.einsum('bnhk,bmhk->bhnm', qi, ki) * decay_intra
        intra = jnp.einsum('bhnm,bmhv->bnhv', s, vi)
        cross = jnp.einsum('bnhk,bhkv->bnhv', qi, R) * q_decay[None, :, :, None]
        R_new = R * chunk_decay[None, :, None, None] + jnp.einsum('bmhk,bmhv,hm->bhkv', ki, vi, k_decay)
        return R_new, intra + cross

    R0 = jnp.zeros((B, H, dk, dv), dt)
    _, out = lax.scan(step, R0, (qc, kc, vc))
    return out.transpose(1, 0, 2, 3, 4).reshape(B, S, H, dv)


def setup_inputs(seed: int = 0) -> dict:
    key = jax.random.key(seed)
    ks = jax.random.split(key, 24)
    f32 = jnp.float32
    nrm = lambda k, shape, s: (jax.random.normal(k, shape, f32) * s)
    gain = lambda k, shape: 1.0 + 0.02 * jax.random.normal(k, shape, f32)
    L = DEPTH
    return {
        "x": nrm(ks[0], (BATCH, SEQ, D_MODEL), 1.0),
        "p": nrm(ks[1], (DEPTH, BATCH, SEQ, PLE_DIM), 1.0),
        "g_mix": gain(ks[2], (L, D_MODEL)),
        "w_in": nrm(ks[3], (L, D_MODEL, D_IN), D_MODEL ** -0.5),
        "lam_q1": nrm(ks[4], (L, DIFF_HEAD_DIM), 0.1),
        "lam_k1": nrm(ks[5], (L, DIFF_HEAD_DIM), 0.1),
        "lam_q2": nrm(ks[6], (L, DIFF_HEAD_DIM), 0.1),
        "lam_k2": nrm(ks[7], (L, DIFF_HEAD_DIM), 0.1),
        "g_diff_sub": gain(ks[8], (L, DIFF_V_DIM)),
        "g_ret_sub": gain(ks[9], (L, N_RET_HEADS, RET_VAL_DIM)),
        "w_branch_diff": nrm(ks[10], (L, DIFF_V_W, D_MODEL), DIFF_V_W ** -0.5),
        "w_branch_ret": nrm(ks[11], (L, RET_V_W, D_MODEL), RET_V_W ** -0.5),
        "w_out": nrm(ks[12], (L, D_MODEL, D_MODEL), D_MODEL ** -0.5),
        "g_mlp": gain(ks[13], (L, D_MODEL)),
        "w_ff1": nrm(ks[14], (L, D_MODEL, D_FF), D_MODEL ** -0.5),
        "w_ff2": nrm(ks[15], (L, D_FF, D_MODEL), D_FF ** -0.5),
        "g_ple": gain(ks[16], (L, D_MODEL)),
        "w_ple_gate": nrm(ks[17], (L, D_MODEL, D_MODEL), D_MODEL ** -0.5),
        "w_ple": nrm(ks[18], (L, PLE_DIM, D_MODEL), PLE_DIM ** -0.5),
        "g_final": gain(ks[19], (D_MODEL,)),
    }


def reference(x, p, g_mix, w_in, lam_q1, lam_k1, lam_q2, lam_k2, g_diff_sub, g_ret_sub,
              w_branch_diff, w_branch_ret, w_out, g_mlp, w_ff1, w_ff2, g_ple, w_ple_gate,
              w_ple, g_final):
    B, S, _ = x.shape
    split_idx = list(np.cumsum(IN_SPLITS)[:-1])
    for i in range(DEPTH):
        h = rms_norm(x, g_mix[i])
        proj = h @ w_in[i]
        qa, ka, va, qr, kr, vr, gr, gate_a, gate_r = jnp.split(proj, split_idx, axis=-1)

        lam_init = 0.8 - 0.6 * math.exp(-0.3 * i)
        lam = (jnp.exp(jnp.sum(lam_q1[i] * lam_k1[i]).astype(jnp.float32))
               - jnp.exp(jnp.sum(lam_q2[i] * lam_k2[i]).astype(jnp.float32)) + lam_init)
        qa = qa.reshape(B, S, N_DIFF_HEADS, 2, DIFF_HEAD_DIM)
        ka = ka.reshape(B, S, N_DIFF_HEADS, 2, DIFF_HEAD_DIM)
        va = va.reshape(B, S, N_DIFF_HEADS, DIFF_V_DIM)
        ya = diff_attention(qa, ka, va, lam)
        ya = (rms_norm(ya, g_diff_sub[i]) * (1.0 - lam_init)).reshape(B, S, DIFF_V_W)

        qr = qr.reshape(B, S, N_RET_HEADS, RET_KEY_DIM)
        kr = kr.reshape(B, S, N_RET_HEADS, RET_KEY_DIM)
        vr = vr.reshape(B, S, N_RET_HEADS, RET_VAL_DIM)
        yr = rms_norm(retention(qr, kr, vr), g_ret_sub[i]).reshape(B, S, RET_V_W)
        yr = jax.nn.silu(gr) * yr

        mixed = (jax.nn.sigmoid(gate_a) * (ya @ w_branch_diff[i])
                 + jax.nn.sigmoid(gate_r) * (yr @ w_branch_ret[i]))
        x = x + mixed @ w_out[i]

        h2 = rms_norm(x, g_mlp[i])
        x = x + jnp.square(jax.nn.relu(h2 @ w_ff1[i])) @ w_ff2[i]

        gate_p = jax.nn.sigmoid(rms_norm(x, g_ple[i]) @ w_ple_gate[i])
        x = x + gate_p * (p[i] @ w_ple[i])
    return rms_norm(x, g_final)
```

```python
import functools
import math

import jax
import jax.numpy as jnp
from jax import lax
from jax.experimental import pallas as pl
from jax.experimental.pallas import tpu as pltpu

F32 = jnp.float32
BF16 = jnp.bfloat16

D_MODEL = 1024
PLE_DIM = 256
N_DIFF_HEADS = 8
DIFF_HEAD_DIM = 64
DIFF_V_DIM = 128
N_RET_HEADS = 4
RET_KEY_DIM = 128
RET_VAL_DIM = 256
D_FF = 4096
D_IN = 8192
EPS = 1e-6
LAM_INIT = 0.8 - 0.6 * math.exp(-0.3 * 0)

QA_BLK, KA_BLK, VA_BLK = 0, 8, 16
QR_BLK, KR_BLK = 24, 28
VR_BLK, GR_BLK = 16, 20
GATE_A_BLK, GATE_R_BLK = 6, 7

ATT_T = 256
RET_C = 256
NEG = -1e30
VMEM_LIMIT = 56 * 1024 * 1024


def _rms(x, g):
    ms = jnp.mean(x * x, axis=-1, keepdims=True)
    return x * lax.rsqrt(ms + EPS) * g


def _in_proj_kernel(x_ref, g_ref, w_ref, o_ref, h_scr):
    @pl.when(pl.program_id(1) == 0)
    def _():
        h_scr[...] = _rms(x_ref[...], g_ref[...]).astype(BF16)

    o_ref[...] = jnp.dot(h_scr[...], w_ref[...], preferred_element_type=F32).astype(BF16)


def _in_proj(x2d, g_mix, w_in_bf16, tm=1024, tn=1024):
    T = x2d.shape[0]
    return pl.pallas_call(
        _in_proj_kernel,
        grid=(T // tm, D_IN // tn),
        in_specs=[
            pl.BlockSpec((tm, D_MODEL), lambda i, j: (i, 0)),
            pl.BlockSpec((1, D_MODEL), lambda i, j: (0, 0)),
            pl.BlockSpec((D_MODEL, tn), lambda i, j: (0, j)),
        ],
        out_specs=pl.BlockSpec((tm, tn), lambda i, j: (i, j)),
        out_shape=jax.ShapeDtypeStruct((T, D_IN), BF16),
        scratch_shapes=[pltpu.VMEM((tm, D_MODEL), BF16)],
        compiler_params=pltpu.CompilerParams(
            dimension_semantics=("arbitrary", "arbitrary"), vmem_limit_bytes=VMEM_LIMIT),
        name="in_proj",
    )(x2d, g_mix, w_in_bf16)


def _attn_kernel(slopes_ref, lq1_ref, lk1_ref, lq2_ref, lk2_ref, g_ref,
                 q_ref, k_ref, v_ref, o_ref,
                 vT_scr, d_scr, acc_scr, m_scr, l_scr):
    T = ATT_T
    S = q_ref.shape[0]
    nq = S // T
    h = pl.program_id(1)
    slope = slopes_ref[h]

    lam = (jnp.exp(jnp.sum(lq1_ref[...] * lk1_ref[...], axis=-1, keepdims=True))
           - jnp.exp(jnp.sum(lq2_ref[...] * lk2_ref[...], axis=-1, keepdims=True))
           + LAM_INIT)

    vT_scr[...] = v_ref[...].T
    r_io = lax.broadcasted_iota(jnp.int32, (T, T), 0)
    c_io = lax.broadcasted_iota(jnp.int32, (T, T), 1)
    d_scr[...] = slope * (r_io - c_io).astype(F32)
    slope_row = jnp.full((1, T), slope, F32)
    lane = lax.broadcasted_iota(jnp.int32, (T, 2 * DIFF_HEAD_DIM), 1)

    def tile(qms, j, off_row, masked):
        ks = pl.multiple_of(j * T, T)
        k_t = k_ref[pl.ds(ks, T), :]
        vT_t = vT_scr[:, pl.ds(ks, T)]
        for c in range(2):
            sT = lax.dot_general(k_t, qms[c], (((1,), (1,)), ((), ())),
                                 preferred_element_type=F32)
            u = sT + d_scr[...]
            if masked:
                u = jnp.where(r_io <= c_io, u, NEG)
            m_old = m_scr[c]
            m_new = jnp.maximum(m_old, jnp.max(u, axis=0, keepdims=True) + off_row)
            alpha = jnp.exp(m_old - m_new)
            p = jnp.exp(u - (m_new - off_row))
            l_scr[c] = alpha * l_scr[c] + jnp.sum(p, axis=0, keepdims=True)
            acc_scr[c] = alpha * acc_scr[c] + jnp.dot(
                vT_t, p.astype(BF16), preferred_element_type=F32)
            m_scr[c] = m_new

    def q_block(qi, carry):
        qs = pl.multiple_of(qi * T, T)
        q = (q_ref[pl.ds(qs, T), :].astype(F32) * (DIFF_HEAD_DIM ** -0.5)).astype(BF16)
        zero = jnp.zeros_like(q)
        qms = (jnp.where(lane < DIFF_HEAD_DIM, q, zero),
               jnp.where(lane >= DIFF_HEAD_DIM, q, zero))
        m_scr[...] = jnp.full(m_scr.shape, NEG, F32)
        l_scr[...] = jnp.zeros(l_scr.shape, F32)
        acc_scr[...] = jnp.zeros(acc_scr.shape, F32)

        def kv_step(j, c2):
            off_row = slope_row * (jnp.full((1, T), (j - qi) * T, jnp.int32).astype(F32))
            tile(qms, j, off_row, masked=False)
            return c2

        lax.fori_loop(0, qi, kv_step, 0)
        tile(qms, qi, jnp.zeros((1, T), F32), masked=True)

        o = acc_scr[0] * (1.0 / l_scr[0]) - lam * (acc_scr[1] * (1.0 / l_scr[1]))
        ms = jnp.mean(o * o, axis=0, keepdims=True)
        y = o * (lax.rsqrt(ms + EPS) * (1.0 - LAM_INIT))
        o_ref[pl.ds(qs, T), :] = (y.T * g_ref[...]).astype(o_ref.dtype)
        return carry

    lax.fori_loop(0, nq, q_block, 0)


def _diff_attn(proj, slopes, lq1, lk1, lq2, lk2, g_sub, B, S):
    T = B * S
    hd = 2 * DIFF_HEAD_DIM
    small = lambda: pl.BlockSpec((1, DIFF_HEAD_DIM), lambda b, h: (0, 0))
    return pl.pallas_call(
        _attn_kernel,
        grid=(B, N_DIFF_HEADS),
        in_specs=[
            pl.BlockSpec(memory_space=pltpu.SMEM),
            small(), small(), small(), small(),
            pl.BlockSpec((1, DIFF_V_DIM), lambda b, h: (0, 0)),
            pl.BlockSpec((S, hd), lambda b, h: (b, QA_BLK + h)),
            pl.BlockSpec((S, hd), lambda b, h: (b, KA_BLK + h)),
            pl.BlockSpec((S, DIFF_V_DIM), lambda b, h: (b, VA_BLK + h)),
        ],
        out_specs=pl.BlockSpec((S, DIFF_V_DIM), lambda b, h: (b, h)),
        out_shape=jax.ShapeDtypeStruct((T, N_DIFF_HEADS * DIFF_V_DIM), BF16),
        scratch_shapes=[
            pltpu.VMEM((DIFF_V_DIM, S), BF16),
            pltpu.VMEM((ATT_T, ATT_T), F32),
            pltpu.VMEM((2, DIFF_V_DIM, ATT_T), F32),
            pltpu.VMEM((2, 1, ATT_T), F32),
            pltpu.VMEM((2, 1, ATT_T), F32),
        ],
        compiler_params=pltpu.CompilerParams(
            dimension_semantics=("arbitrary", "arbitrary"), vmem_limit_bytes=VMEM_LIMIT),
        name="diff_attn",
    )(slopes, lq1, lk1, lq2, lk2, g_sub, proj, proj, proj)


def _ret_kernel(cdec_ref, dmat_ref, qdec_ref, kdec_ref, g_ref,
                q_ref, k_ref, v_ref, gr_ref, o_ref, r_scr):
    C = RET_C
    S = q_ref.shape[0]
    h = pl.program_id(1)
    cd = cdec_ref[h]
    r_scr[...] = jnp.zeros(r_scr.shape, F32)

    def chunk(i, carry):
        cs = pl.multiple_of(i * C, C)
        qi = q_ref[pl.ds(cs, C), :]
        ki = k_ref[pl.ds(cs, C), :]
        vi = v_ref[pl.ds(cs, C), :]
        s = lax.dot_general(qi, ki, (((1,), (1,)), ((), ())),
                            preferred_element_type=F32) * dmat_ref[0]
        intra = jnp.dot(s.astype(BF16), vi, preferred_element_type=F32)
        r_old = r_scr[...]
        cross = jnp.dot(qi, r_old.astype(BF16), preferred_element_type=F32) * qdec_ref[0]
        kd = (ki.astype(F32) * kdec_ref[0]).astype(BF16)
        r_scr[...] = r_old * cd + jnp.dot(kd.T, vi, preferred_element_type=F32)
        y = _rms(intra + cross, g_ref[0])
        gr = gr_ref[pl.ds(cs, C), :].astype(F32)
        o_ref[pl.ds(cs, C), :] = (y * (gr * jax.nn.sigmoid(gr))).astype(o_ref.dtype)
        return carry

    lax.fori_loop(0, S // C, chunk, 0)


def _retention(proj, g_ret_sub, B, S):
    T = B * S
    C = RET_C
    hh = jnp.arange(N_RET_HEADS, dtype=F32)
    log_gamma = jnp.log(1.0 - 2.0 ** (-5.0 - hh))
    pos = jnp.arange(C, dtype=F32)
    rel = pos[:, None] - pos[None, :]
    kscale = RET_KEY_DIM ** -0.5
    dmat = jnp.where(rel >= 0, jnp.exp(log_gamma[:, None, None] * rel), 0.0) * kscale
    qdec = jnp.broadcast_to(jnp.exp(log_gamma[:, None] * (pos[None, :] + 1.0))[:, :, None],
                            (N_RET_HEADS, C, RET_VAL_DIM))
    kdec = jnp.broadcast_to((jnp.exp(log_gamma[:, None] * (C - 1.0 - pos[None, :])) * kscale)[:, :, None],
                            (N_RET_HEADS, C, RET_KEY_DIM))
    cdec = jnp.exp(log_gamma * C)
    return pl.pallas_call(
        _ret_kernel,
        grid=(B, N_RET_HEADS),
        in_specs=[
            pl.BlockSpec(memory_space=pltpu.SMEM),
            pl.BlockSpec((1, C, C), lambda b, h: (h, 0, 0)),
            pl.BlockSpec((1, C, RET_VAL_DIM), lambda b, h: (h, 0, 0)),
            pl.BlockSpec((1, C, RET_KEY_DIM), lambda b, h: (h, 0, 0)),
            pl.BlockSpec((1, 1, RET_VAL_DIM), lambda b, h: (h, 0, 0)),
            pl.BlockSpec((S, RET_KEY_DIM), lambda b, h: (b, QR_BLK + h)),
            pl.BlockSpec((S, RET_KEY_DIM), lambda b, h: (b, KR_BLK + h)),
            pl.BlockSpec((S, RET_VAL_DIM), lambda b, h: (b, VR_BLK + h)),
            pl.BlockSpec((S, RET_VAL_DIM), lambda b, h: (b, GR_BLK + h)),
        ],
        out_specs=pl.BlockSpec((S, RET_VAL_DIM), lambda b, h: (b, h)),
        out_shape=jax.ShapeDtypeStruct((T, N_RET_HEADS * RET_VAL_DIM), BF16),
        scratch_shapes=[pltpu.VMEM((RET_KEY_DIM, RET_VAL_DIM), F32)],
        compiler_params=pltpu.CompilerParams(
            dimension_semantics=("arbitrary", "arbitrary"), vmem_limit_bytes=VMEM_LIMIT),
        name="retention",
    )(cdec, dmat, qdec, kdec, g_ret_sub.reshape(N_RET_HEADS, 1, RET_VAL_DIM),
      proj, proj, proj, proj)


def _merge_kernel(x_ref, ya_ref, yr_ref, ga_ref, gr_ref, wbd_ref, wbr_ref, wo_ref, o_ref):
    a = jnp.dot(ya_ref[...], wbd_ref[...], preferred_element_type=F32)
    r = jnp.dot(yr_ref[...], wbr_ref[...], preferred_element_type=F32)
    mixed = (jax.nn.sigmoid(ga_ref[...].astype(F32)) * a
             + jax.nn.sigmoid(gr_ref[...].astype(F32)) * r)
    o_ref[...] = x_ref[...] + jnp.dot(mixed.astype(BF16), wo_ref[...],
                                      preferred_element_type=F32)


def _merge(x2d, ya, yr, proj, wbd, wbr, wo, tm=512):
    T = x2d.shape[0]
    row = lambda c: pl.BlockSpec((tm, D_MODEL), lambda i: (i, c))
    wspec = lambda: pl.BlockSpec((D_MODEL, D_MODEL), lambda i: (0, 0))
    return pl.pallas_call(
        _merge_kernel,
        grid=(T // tm,),
        in_specs=[row(0), row(0), row(0), row(GATE_A_BLK), row(GATE_R_BLK),
                  wspec(), wspec(), wspec()],
        out_specs=row(0),
        out_shape=jax.ShapeDtypeStruct((T, D_MODEL), F32),
        compiler_params=pltpu.CompilerParams(
            dimension_semantics=("arbitrary",), vmem_limit_bytes=VMEM_LIMIT),
        name="merge",
    )(x2d, ya, yr, proj, proj, wbd, wbr, wo)


def _mlp_kernel(x_ref, p_ref, gm_ref, w1_ref, w2_ref, gp_ref, wpg_ref, wple_ref, gf_ref, o_ref):
    x1 = x_ref[...]
    h2 = _rms(x1, gm_ref[...]).astype(BF16)
    acc = jnp.zeros_like(x1)
    for f in range(D_FF // D_MODEL):
        sl = slice(f * D_MODEL, (f + 1) * D_MODEL)
        a = jnp.maximum(jnp.dot(h2, w1_ref[:, sl], preferred_element_type=F32), 0.0)
        acc = acc + jnp.dot((a * a).astype(BF16), w2_ref[sl, :], preferred_element_type=F32)
    x2 = x1 + acc
    hp = _rms(x2, gp_ref[...]).astype(BF16)
    gate = jax.nn.sigmoid(jnp.dot(hp, wpg_ref[...], preferred_element_type=F32))
    pe = jnp.dot(p_ref[...].astype(BF16), wple_ref[...], preferred_element_type=F32)
    x3 = x2 + gate * pe
    o_ref[...] = _rms(x3, gf_ref[...])


def _mlp_ple(x1, p2d, g_mlp, w1, w2, g_ple, wpg, wple, g_final, tm=512):
    T = x1.shape[0]
    row = pl.BlockSpec((tm, D_MODEL), lambda i: (i, 0))
    vec = lambda: pl.BlockSpec((1, D_MODEL), lambda i: (0, 0))
    res = lambda shape: pl.BlockSpec(shape, lambda i: (0, 0), pipeline_mode=pl.Buffered(1))
    return pl.pallas_call(
        _mlp_kernel,
        grid=(T // tm,),
        in_specs=[row, pl.BlockSpec((tm, PLE_DIM), lambda i: (i, 0)), vec(),
                  res((D_MODEL, D_FF)), res((D_FF, D_MODEL)), vec(),
                  res((D_MODEL, D_MODEL)), res((PLE_DIM, D_MODEL)), vec()],
        out_specs=row,
        out_shape=jax.ShapeDtypeStruct((T, D_MODEL), F32),
        compiler_params=pltpu.CompilerParams(
            dimension_semantics=("arbitrary",), vmem_limit_bytes=VMEM_LIMIT),
        name="mlp_ple",
    )(x1, p2d, g_mlp, w1, w2, g_ple, wpg, wple, g_final)


def kernel(x, p, g_mix, w_in, lam_q1, lam_k1, lam_q2, lam_k2, g_diff_sub, g_ret_sub,
           w_branch_diff, w_branch_ret, w_out, g_mlp, w_ff1, w_ff2, g_ple, w_ple_gate,
           w_ple, g_final):
    B, S, D = x.shape
    T = B * S
    x2d = x.reshape(T, D)
    slopes = 2.0 ** (-8.0 * jnp.arange(1, N_DIFF_HEADS + 1, dtype=F32) / N_DIFF_HEADS)
    bf = lambda w: w.astype(BF16)

    proj = _in_proj(x2d, g_mix[0][None, :], bf(w_in[0]))
    ya = _diff_attn(proj, slopes, lam_q1[0][None, :], lam_k1[0][None, :],
                    lam_q2[0][None, :], lam_k2[0][None, :], g_diff_sub[0][None, :], B, S)
    yr = _retention(proj, g_ret_sub[0], B, S)
    x1 = _merge(x2d, ya, yr, proj, bf(w_branch_diff[0]), bf(w_branch_ret[0]), bf(w_out[0]))
    out = _mlp_ple(x1, p[0].reshape(T, PLE_DIM), g_mlp[0][None, :], bf(w_ff1[0]), bf(w_ff2[0]),
                   g_ple[0][None, :], bf(w_ple_gate[0]), bf(w_ple[0]), g_final[None, :])
    return out.reshape(B, S, D)
```

```python
import functools
import math

import jax
import jax.numpy as jnp
from jax import lax
from jax.experimental import pallas as pl
from jax.experimental.pallas import tpu as pltpu

F32 = jnp.float32
BF16 = jnp.bfloat16

D_MODEL = 1024
PLE_DIM = 256
N_DIFF_HEADS = 8
DIFF_HEAD_DIM = 64
DIFF_V_DIM = 128
N_RET_HEADS = 4
RET_KEY_DIM = 128
RET_VAL_DIM = 256
D_FF = 4096
D_IN = 8192
EPS = 1e-6
LAM_INIT = 0.8 - 0.6 * math.exp(-0.3 * 0)

QA_BLK, KA_BLK, VA_BLK = 0, 8, 16
QR_BLK, KR_BLK = 24, 28
VR_BLK, GR_BLK = 16, 20
GATE_A_BLK, GATE_R_BLK = 6, 7

ATT_T = 256
RET_C = 256
NEG = -1e30
VMEM_LIMIT = 56 * 1024 * 1024


def _rms(x, g):
    ms = jnp.mean(x * x, axis=-1, keepdims=True)
    return x * lax.rsqrt(ms + EPS) * g


def _in_proj_kernel(x_ref, g_ref, w_ref, o_ref, *, tn):
    h = _rms(x_ref[...], g_ref[...]).astype(BF16)
    for n in range(D_IN // tn):
        sl = slice(n * tn, (n + 1) * tn)
        o_ref[:, sl] = jnp.dot(h, w_ref[:, sl], preferred_element_type=F32).astype(BF16)


def _in_proj(x2d, g_mix, w_in_bf16, tm=512, tn=1024):
    T = x2d.shape[0]
    return pl.pallas_call(
        functools.partial(_in_proj_kernel, tn=tn),
        grid=(T // tm,),
        in_specs=[
            pl.BlockSpec((tm, D_MODEL), lambda i: (i, 0)),
            pl.BlockSpec((1, D_MODEL), lambda i: (0, 0)),
            pl.BlockSpec((D_MODEL, D_IN), lambda i: (0, 0), pipeline_mode=pl.Buffered(1)),
        ],
        out_specs=pl.BlockSpec((tm, D_IN), lambda i: (i, 0)),
        out_shape=jax.ShapeDtypeStruct((T, D_IN), BF16),
        compiler_params=pltpu.CompilerParams(
            dimension_semantics=("arbitrary",), vmem_limit_bytes=VMEM_LIMIT),
        name="in_proj",
    )(x2d, g_mix, w_in_bf16)


def _attn_kernel(slopes_ref, lq1_ref, lk1_ref, lq2_ref, lk2_ref, g_ref,
                 q_ref, k_ref, v_ref, o_ref, vT_scr, b_scr):
    T = ATT_T
    S = q_ref.shape[0]
    nq = S // T
    h = pl.program_id(1)
    slope = slopes_ref[h]

    lam = (jnp.exp(jnp.sum(lq1_ref[...] * lk1_ref[...], axis=-1, keepdims=True))
           - jnp.exp(jnp.sum(lq2_ref[...] * lk2_ref[...], axis=-1, keepdims=True))
           + LAM_INIT)

    vT_scr[...] = v_ref[...].T
    b_scr[...] = slope * (lax.broadcasted_iota(jnp.int32, (S, T), 0)
                          - lax.broadcasted_iota(jnp.int32, (S, T), 1)).astype(F32)
    causal = (lax.broadcasted_iota(jnp.int32, (T, T), 0)
              <= lax.broadcasted_iota(jnp.int32, (T, T), 1))
    lane = lax.broadcasted_iota(jnp.int32, (T, 2 * DIFF_HEAD_DIM), 1)
    nt = (((1,), (1,)), ((), ()))

    for qi in range(nq):
        k0 = qi * T
        q = (q_ref[k0:k0 + T, :].astype(F32) * (DIFF_HEAD_DIM ** -0.5)).astype(BF16)
        zero = jnp.zeros_like(q)
        qms = (jnp.where(lane < DIFF_HEAD_DIM, q, zero),
               jnp.where(lane >= DIFF_HEAD_DIM, q, zero))
        outs = []
        for c in range(2):
            s_d = lax.dot_general(k_ref[k0:k0 + T, :], qms[c], nt,
                                  preferred_element_type=F32) + b_scr[k0:k0 + T, :]
            s_d = jnp.where(causal, s_d, NEG)
            m = jnp.max(s_d, axis=0, keepdims=True)
            if qi > 0:
                s_a = lax.dot_general(k_ref[0:k0, :], qms[c], nt,
                                      preferred_element_type=F32) + b_scr[0:k0, :]
                m = jnp.maximum(m, jnp.max(s_a, axis=0, keepdims=True))
            p_d = jnp.exp(s_d - m)
            l = jnp.sum(p_d, axis=0, keepdims=True)
            acc = jnp.dot(vT_scr[:, k0:k0 + T], p_d.astype(BF16),
                          preferred_element_type=F32)
            if qi > 0:
                p_a = jnp.exp(s_a - m)
                l = l + jnp.sum(p_a, axis=0, keepdims=True)
                acc = acc + jnp.dot(vT_scr[:, 0:k0], p_a.astype(BF16),
                                    preferred_element_type=F32)
            outs.append(acc * (1.0 / l))
        o = outs[0] - lam * outs[1]
        ms = jnp.mean(o * o, axis=0, keepdims=True)
        y = o * (lax.rsqrt(ms + EPS) * (1.0 - LAM_INIT))
        o_ref[k0:k0 + T, :] = (y.T * g_ref[...]).astype(o_ref.dtype)


def _diff_attn(proj, slopes, lq1, lk1, lq2, lk2, g_sub, B, S):
    T = B * S
    hd = 2 * DIFF_HEAD_DIM
    small = lambda: pl.BlockSpec((1, DIFF_HEAD_DIM), lambda b, h: (0, 0))
    return pl.pallas_call(
        _attn_kernel,
        grid=(B, N_DIFF_HEADS),
        in_specs=[
            pl.BlockSpec(memory_space=pltpu.SMEM),
            small(), small(), small(), small(),
            pl.BlockSpec((1, DIFF_V_DIM), lambda b, h: (0, 0)),
            pl.BlockSpec((S, hd), lambda b, h: (b, QA_BLK + h)),
            pl.BlockSpec((S, hd), lambda b, h: (b, KA_BLK + h)),
            pl.BlockSpec((S, DIFF_V_DIM), lambda b, h: (b, VA_BLK + h)),
        ],
        out_specs=pl.BlockSpec((S, DIFF_V_DIM), lambda b, h: (b, h)),
        out_shape=jax.ShapeDtypeStruct((T, N_DIFF_HEADS * DIFF_V_DIM), BF16),
        scratch_shapes=[
            pltpu.VMEM((DIFF_V_DIM, S), BF16),
            pltpu.VMEM((S, ATT_T), F32),
        ],
        compiler_params=pltpu.CompilerParams(
            dimension_semantics=("arbitrary", "arbitrary"), vmem_limit_bytes=VMEM_LIMIT),
        name="diff_attn",
    )(slopes, lq1, lk1, lq2, lk2, g_sub, proj, proj, proj)


def _ret_kernel(cdec_ref, dmat_ref, qdec_ref, kdec_ref, g_ref,
                q_ref, k_ref, v_ref, gr_ref, o_ref, r_scr):
    C = RET_C
    S = q_ref.shape[0]
    h = pl.program_id(1)
    cd = cdec_ref[h]
    r_scr[...] = jnp.zeros(r_scr.shape, F32)

    def chunk(i, carry):
        cs = pl.multiple_of(i * C, C)
        qi = q_ref[pl.ds(cs, C), :]
        ki = k_ref[pl.ds(cs, C), :]
        vi = v_ref[pl.ds(cs, C), :]
        s = lax.dot_general(qi, ki, (((1,), (1,)), ((), ())),
                            preferred_element_type=F32) * dmat_ref[0]
        intra = jnp.dot(s.astype(BF16), vi, preferred_element_type=F32)
        r_old = r_scr[...]
        cross = jnp.dot(qi, r_old.astype(BF16), preferred_element_type=F32) * qdec_ref[0]
        kd = (ki.astype(F32) * kdec_ref[0]).astype(BF16)
        r_scr[...] = r_old * cd + jnp.dot(kd.T, vi, preferred_element_type=F32)
        y = _rms(intra + cross, g_ref[0])
        gr = gr_ref[pl.ds(cs, C), :].astype(F32)
        o_ref[pl.ds(cs, C), :] = (y * (gr * jax.nn.sigmoid(gr))).astype(o_ref.dtype)
        return carry

    lax.fori_loop(0, S // C, chunk, 0)


def _retention(proj, g_ret_sub, B, S):
    T = B * S
    C = RET_C
    hh = jnp.arange(N_RET_HEADS, dtype=F32)
    log_gamma = jnp.log(1.0 - 2.0 ** (-5.0 - hh))
    pos = jnp.arange(C, dtype=F32)
    rel = pos[:, None] - pos[None, :]
    kscale = RET_KEY_DIM ** -0.5
    dmat = jnp.where(rel >= 0, jnp.exp(log_gamma[:, None, None] * rel), 0.0) * kscale
    qdec = jnp.broadcast_to(jnp.exp(log_gamma[:, None] * (pos[None, :] + 1.0))[:, :, None],
                            (N_RET_HEADS, C, RET_VAL_DIM))
    kdec = jnp.broadcast_to((jnp.exp(log_gamma[:, None] * (C - 1.0 - pos[None, :])) * kscale)[:, :, None],
                            (N_RET_HEADS, C, RET_KEY_DIM))
    cdec = jnp.exp(log_gamma * C)
    return pl.pallas_call(
        _ret_kernel,
        grid=(B, N_RET_HEADS),
        in_specs=[
            pl.BlockSpec(memory_space=pltpu.SMEM),
            pl.BlockSpec((1, C, C), lambda b, h: (h, 0, 0)),
            pl.BlockSpec((1, C, RET_VAL_DIM), lambda b, h: (h, 0, 0)),
            pl.BlockSpec((1, C, RET_KEY_DIM), lambda b, h: (h, 0, 0)),
            pl.BlockSpec((1, 1, RET_VAL_DIM), lambda b, h: (h, 0, 0)),
            pl.BlockSpec((S, RET_KEY_DIM), lambda b, h: (b, QR_BLK + h)),
            pl.BlockSpec((S, RET_KEY_DIM), lambda b, h: (b, KR_BLK + h)),
            pl.BlockSpec((S, RET_VAL_DIM), lambda b, h: (b, VR_BLK + h)),
            pl.BlockSpec((S, RET_VAL_DIM), lambda b, h: (b, GR_BLK + h)),
        ],
        out_specs=pl.BlockSpec((S, RET_VAL_DIM), lambda b, h: (b, h)),
        out_shape=jax.ShapeDtypeStruct((T, N_RET_HEADS * RET_VAL_DIM), BF16),
        scratch_shapes=[pltpu.VMEM((RET_KEY_DIM, RET_VAL_DIM), F32)],
        compiler_params=pltpu.CompilerParams(
            dimension_semantics=("arbitrary", "arbitrary"), vmem_limit_bytes=VMEM_LIMIT),
        name="retention",
    )(cdec, dmat, qdec, kdec, g_ret_sub.reshape(N_RET_HEADS, 1, RET_VAL_DIM),
      proj, proj, proj, proj)


def _merge_kernel(x_ref, ya_ref, yr_ref, ga_ref, gr_ref, wbd_ref, wbr_ref, wo_ref, o_ref):
    a = jnp.dot(ya_ref[...], wbd_ref[...], preferred_element_type=F32)
    r = jnp.dot(yr_ref[...], wbr_ref[...], preferred_element_type=F32)
    mixed = (jax.nn.sigmoid(ga_ref[...].astype(F32)) * a
             + jax.nn.sigmoid(gr_ref[...].astype(F32)) * r)
    o_ref[...] = x_ref[...] + jnp.dot(mixed.astype(BF16), wo_ref[...],
                                      preferred_element_type=F32)


def _merge(x2d, ya, yr, proj, wbd, wbr, wo, tm=512):
    T = x2d.shape[0]
    row = lambda c: pl.BlockSpec((tm, D_MODEL), lambda i: (i, c))
    wspec = lambda: pl.BlockSpec((D_MODEL, D_MODEL), lambda i: (0, 0))
    return pl.pallas_call(
        _merge_kernel,
        grid=(T // tm,),
        in_specs=[row(0), row(0), row(0), row(GATE_A_BLK), row(GATE_R_BLK),
                  wspec(), wspec(), wspec()],
        out_specs=row(0),
        out_shape=jax.ShapeDtypeStruct((T, D_MODEL), F32),
        compiler_params=pltpu.CompilerParams(
            dimension_semantics=("arbitrary",), vmem_limit_bytes=VMEM_LIMIT),
        name="merge",
    )(x2d, ya, yr, proj, proj, wbd, wbr, wo)


def _mlp_kernel(x_ref, p_ref, gm_ref, w1_ref, w2_ref, gp_ref, wpg_ref, wple_ref, gf_ref, o_ref):
    x1 = x_ref[...]
    h2 = _rms(x1, gm_ref[...]).astype(BF16)
    acc = jnp.zeros_like(x1)
    for f in range(D_FF // D_MODEL):
        sl = slice(f * D_MODEL, (f + 1) * D_MODEL)
        a = jnp.maximum(jnp.dot(h2, w1_ref[:, sl], preferred_element_type=F32), 0.0)
        acc = acc + jnp.dot((a * a).astype(BF16), w2_ref[sl, :], preferred_element_type=F32)
    x2 = x1 + acc
    hp = _rms(x2, gp_ref[...]).astype(BF16)
    gate = jax.nn.sigmoid(jnp.dot(hp, wpg_ref[...], preferred_element_type=F32))
    pe = jnp.dot(p_ref[...].astype(BF16), wple_ref[...], preferred_element_type=F32)
    x3 = x2 + gate * pe
    o_ref[...] = _rms(x3, gf_ref[...])


def _mlp_ple(x1, p2d, g_mlp, w1, w2, g_ple, wpg, wple, g_final, tm=512):
    T = x1.shape[0]
    row = pl.BlockSpec((tm, D_MODEL), lambda i: (i, 0))
    vec = lambda: pl.BlockSpec((1, D_MODEL), lambda i: (0, 0))
    res = lambda shape: pl.BlockSpec(shape, lambda i: (0, 0), pipeline_mode=pl.Buffered(1))
    return pl.pallas_call(
        _mlp_kernel,
        grid=(T // tm,),
        in_specs=[row, pl.BlockSpec((tm, PLE_DIM), lambda i: (i, 0)), vec(),
                  res((D_MODEL, D_FF)), res((D_FF, D_MODEL)), vec(),
                  res((D_MODEL, D_MODEL)), res((PLE_DIM, D_MODEL)), vec()],
        out_specs=row,
        out_shape=jax.ShapeDtypeStruct((T, D_MODEL), F32),
        compiler_params=pltpu.CompilerParams(
            dimension_semantics=("arbitrary",), vmem_limit_bytes=VMEM_LIMIT),
        name="mlp_ple",
    )(x1, p2d, g_mlp, w1, w2, g_ple, wpg, wple, g_final)


def kernel(x, p, g_mix, w_in, lam_q1, lam_k1, lam_q2, lam_k2, g_diff_sub, g_ret_sub,
           w_branch_diff, w_branch_ret, w_out, g_mlp, w_ff1, w_ff2, g_ple, w_ple_gate,
           w_ple, g_final):
    B, S, D = x.shape
    T = B * S
    x2d = x.reshape(T, D)
    slopes = 2.0 ** (-8.0 * jnp.arange(1, N_DIFF_HEADS + 1, dtype=F32) / N_DIFF_HEADS)
    bf = lambda w: w.astype(BF16)

    proj = _in_proj(x2d, g_mix[0][None, :], bf(w_in[0]))
    ya = _diff_attn(proj, slopes, lam_q1[0][None, :], lam_k1[0][None, :],
                    lam_q2[0][None, :], lam_k2[0][None, :], g_diff_sub[0][None, :], B, S)
    yr = _retention(proj, g_ret_sub[0], B, S)
    x1 = _merge(x2d, ya, yr, proj, bf(w_branch_diff[0]), bf(w_branch_ret[0]), bf(w_out[0]))
    out = _mlp_ple(x1, p[0].reshape(T, PLE_DIM), g_mlp[0][None, :], bf(w_ff1[0]), bf(w_ff2[0]),
                   g_ple[0][None, :], bf(w_ple_gate[0]), bf(w_ple[0]), g_final[None, :])
    return out.reshape(B, S, D)
```

```python
import functools
import math

import jax
import jax.numpy as jnp
from jax import lax
from jax.experimental import pallas as pl
from jax.experimental.pallas import tpu as pltpu

F32 = jnp.float32
BF16 = jnp.bfloat16

D_MODEL = 1024
PLE_DIM = 256
N_DIFF_HEADS = 8
DIFF_HEAD_DIM = 64
DIFF_V_DIM = 128
N_RET_HEADS = 4
RET_KEY_DIM = 128
RET_VAL_DIM = 256
D_FF = 4096
D_IN = 8192
EPS = 1e-6
LAM_INIT = 0.8 - 0.6 * math.exp(-0.3 * 0)

QA_BLK, KA_BLK, VA_BLK = 0, 8, 16
QR_BLK, KR_BLK = 24, 28
VR_BLK, GR_BLK = 16, 20
GATE_A_BLK, GATE_R_BLK = 6, 7

ATT_T = 256
RET_C = 256
NEG = -1e30
LOG2E = math.log2(math.e)
ALIBI_PIECES = 3
VMEM_LIMIT = 56 * 1024 * 1024


def _rms(x, g):
    ms = jnp.mean(x * x, axis=-1, keepdims=True)
    return x * lax.rsqrt(ms + EPS) * g


def _in_proj_kernel(x_ref, g_ref, w_ref, o_ref, *, tn):
    h = _rms(x_ref[...], g_ref[...]).astype(BF16)
    for n in range(D_IN // tn):
        sl = slice(n * tn, (n + 1) * tn)
        o_ref[:, sl] = jnp.dot(h, w_ref[:, sl], preferred_element_type=F32).astype(BF16)


def _in_proj(x2d, g_mix, w_in_bf16, tm=512, tn=1024):
    T = x2d.shape[0]
    return pl.pallas_call(
        functools.partial(_in_proj_kernel, tn=tn),
        grid=(T // tm,),
        in_specs=[
            pl.BlockSpec((tm, D_MODEL), lambda i: (i, 0)),
            pl.BlockSpec((1, D_MODEL), lambda i: (0, 0)),
            pl.BlockSpec((D_MODEL, D_IN), lambda i: (0, 0), pipeline_mode=pl.Buffered(1)),
        ],
        out_specs=pl.BlockSpec((tm, D_IN), lambda i: (i, 0)),
        out_shape=jax.ShapeDtypeStruct((T, D_IN), BF16),
        compiler_params=pltpu.CompilerParams(
            dimension_semantics=("arbitrary",), vmem_limit_bytes=VMEM_LIMIT),
        name="in_proj",
    )(x2d, g_mix, w_in_bf16)


def _attn_kernel(lq1_ref, lk1_ref, lq2_ref, lk2_ref, g_ref, kb_ref,
                 q_ref, k_ref, v_ref, o_ref, vT_scr):
    T = ATT_T
    S = q_ref.shape[0]
    nq = S // T
    hd = 2 * DIFF_HEAD_DIM

    lam = (jnp.exp(jnp.sum(lq1_ref[...] * lk1_ref[...], axis=-1, keepdims=True))
           - jnp.exp(jnp.sum(lq2_ref[...] * lk2_ref[...], axis=-1, keepdims=True))
           + LAM_INIT)

    vT_scr[...] = v_ref[...].T
    causal = (lax.broadcasted_iota(jnp.int32, (T, 2 * T), 0)
              <= (lax.broadcasted_iota(jnp.int32, (T, 2 * T), 1) & (T - 1)))
    lane = lax.broadcasted_iota(jnp.int32, (T, hd), 1)
    ones_cols = jnp.where(lane < ALIBI_PIECES, 1.0, 0.0).astype(BF16)
    nt = (((1,), (1,)), ((), ()))

    def query_operand(qi):
        q = q_ref[qi * T:(qi + 1) * T, :].astype(F32) * (LOG2E * DIFF_HEAD_DIM ** -0.5)
        q = q.astype(BF16)
        zero = jnp.zeros_like(q)
        qa = jnp.concatenate([jnp.where(lane < DIFF_HEAD_DIM, q, zero), ones_cols], axis=1)
        qb = jnp.concatenate([jnp.where(lane >= DIFF_HEAD_DIM, q, zero), ones_cols], axis=1)
        return jnp.concatenate([qa, qb], axis=0)

    def stage_a(qi):
        n = (qi + 1) * T
        k_aug = jnp.concatenate([k_ref[0:n, :], kb_ref[0, 0:n, :]], axis=1)
        u = lax.dot_general(k_aug, query_operand(qi), nt, preferred_element_type=F32)
        tiles = [u[j * T:(j + 1) * T, :] for j in range(qi)]
        tiles.append(jnp.where(causal, u[qi * T:, :], NEG))
        m = jnp.max(tiles[-1], axis=0, keepdims=True)
        if qi > 0:
            m = jnp.maximum(m, jnp.max(u[:qi * T, :], axis=0, keepdims=True))
        return tiles, m

    def stage_b(qi, tiles, m):
        l = jnp.zeros((1, 2 * T), F32)
        acc = jnp.zeros((DIFF_V_DIM, 2 * T), F32)
        for j, u in enumerate(tiles):
            p = jnp.exp2(u - m)
            l = l + jnp.sum(p, axis=0, keepdims=True)
            acc = acc + jnp.dot(vT_scr[:, j * T:(j + 1) * T], p.astype(BF16),
                                preferred_element_type=F32)
        r = acc * (1.0 / l)
        o = r[:, :T] - lam * r[:, T:]
        ms = jnp.mean(o * o, axis=0, keepdims=True)
        y = o * (lax.rsqrt(ms + EPS) * (1.0 - LAM_INIT))
        o_ref[qi * T:(qi + 1) * T, :] = (y.T * g_ref[...]).astype(o_ref.dtype)

    pending = None
    for qi in range(nq):
        cur = stage_a(qi)
        if pending is not None:
            stage_b(qi - 1, *pending)
        pending = cur
    stage_b(nq - 1, *pending)


def _alibi_key_table(S):
    slopes = 2.0 ** (-8.0 * jnp.arange(1, N_DIFF_HEADS + 1, dtype=F32) / N_DIFF_HEADS)
    rem = (LOG2E * slopes)[:, None] * jnp.arange(S, dtype=F32)[None, :]
    pieces = []
    for _ in range(ALIBI_PIECES):
        piece = lax.bitcast_convert_type(
            lax.bitcast_convert_type(rem, jnp.uint32) & jnp.uint32(0xFFFF0000), F32)
        pieces.append(piece.astype(BF16))
        rem = rem - piece
    tab = jnp.stack(pieces, axis=-1)
    return jnp.pad(tab, ((0, 0), (0, 0), (0, 2 * DIFF_HEAD_DIM - ALIBI_PIECES)))


def _diff_attn(proj, lq1, lk1, lq2, lk2, g_sub, B, S):
    T = B * S
    hd = 2 * DIFF_HEAD_DIM
    small = lambda: pl.BlockSpec((1, DIFF_HEAD_DIM), lambda b, h: (0, 0))
    return pl.pallas_call(
        _attn_kernel,
        grid=(B, N_DIFF_HEADS),
        in_specs=[
            small(), small(), small(), small(),
            pl.BlockSpec((1, DIFF_V_DIM), lambda b, h: (0, 0)),
            pl.BlockSpec((1, S, hd), lambda b, h: (h, 0, 0)),
            pl.BlockSpec((S, hd), lambda b, h: (b, QA_BLK + h)),
            pl.BlockSpec((S, hd), lambda b, h: (b, KA_BLK + h)),
            pl.BlockSpec((S, DIFF_V_DIM), lambda b, h: (b, VA_BLK + h)),
        ],
        out_specs=pl.BlockSpec((S, DIFF_V_DIM), lambda b, h: (b, h)),
        out_shape=jax.ShapeDtypeStruct((T, N_DIFF_HEADS * DIFF_V_DIM), BF16),
        scratch_shapes=[
            pltpu.VMEM((DIFF_V_DIM, S), BF16),
        ],
        compiler_params=pltpu.CompilerParams(
            dimension_semantics=("arbitrary", "arbitrary"), vmem_limit_bytes=VMEM_LIMIT),
        name="diff_attn",
    )(lq1, lk1, lq2, lk2, g_sub, _alibi_key_table(S), proj, proj, proj)


def _ret_kernel(cdec_ref, dmat_ref, qdec_ref, kdec_ref, g_ref,
                q_ref, k_ref, v_ref, gr_ref, o_ref):
    C = RET_C
    S = q_ref.shape[0]
    cd = cdec_ref[pl.program_id(1)]
    nt = (((1,), (1,)), ((), ()))
    r = jnp.zeros((RET_KEY_DIM, RET_VAL_DIM), F32)
    for i in range(S // C):
        rows = slice(i * C, (i + 1) * C)
        qi = q_ref[rows, :]
        ki = k_ref[rows, :]
        vi = v_ref[rows, :]
        s = lax.dot_general(qi, ki, nt, preferred_element_type=F32) * dmat_ref[0]
        o = jnp.dot(s.astype(BF16), vi, preferred_element_type=F32)
        if i > 0:
            o = o + jnp.dot(qi, r.astype(BF16), preferred_element_type=F32) * qdec_ref[0]
        if i + 1 < S // C:
            kd = (ki.astype(F32) * kdec_ref[0]).astype(BF16)
            r = r * cd + jnp.dot(kd.T, vi, preferred_element_type=F32)
        y = _rms(o, g_ref[0])
        gr = gr_ref[rows, :].astype(F32)
        o_ref[rows, :] = (y * (gr * jax.nn.sigmoid(gr))).astype(o_ref.dtype)


def _retention(proj, g_ret_sub, B, S):
    T = B * S
    C = RET_C
    hh = jnp.arange(N_RET_HEADS, dtype=F32)
    log_gamma = jnp.log(1.0 - 2.0 ** (-5.0 - hh))
    pos = jnp.arange(C, dtype=F32)
    rel = pos[:, None] - pos[None, :]
    kscale = RET_KEY_DIM ** -0.5
    dmat = jnp.where(rel >= 0, jnp.exp(log_gamma[:, None, None] * rel), 0.0) * kscale
    qdec = jnp.broadcast_to(jnp.exp(log_gamma[:, None] * (pos[None, :] + 1.0))[:, :, None],
                            (N_RET_HEADS, C, RET_VAL_DIM))
    kdec = jnp.broadcast_to((jnp.exp(log_gamma[:, None] * (C - 1.0 - pos[None, :])) * kscale)[:, :, None],
                            (N_RET_HEADS, C, RET_KEY_DIM))
    cdec = jnp.exp(log_gamma * C)
    return pl.pallas_call(
        _ret_kernel,
        grid=(B, N_RET_HEADS),
        in_specs=[
            pl.BlockSpec(memory_space=pltpu.SMEM),
            pl.BlockSpec((1, C, C), lambda b, h: (h, 0, 0)),
            pl.BlockSpec((1, C, RET_VAL_DIM), lambda b, h: (h, 0, 0)),
            pl.BlockSpec((1, C, RET_KEY_DIM), lambda b, h: (h, 0, 0)),
            pl.BlockSpec((1, 1, RET_VAL_DIM), lambda b, h: (h, 0, 0)),
            pl.BlockSpec((S, RET_KEY_DIM), lambda b, h: (b, QR_BLK + h)),
            pl.BlockSpec((S, RET_KEY_DIM), lambda b, h: (b, KR_BLK + h)),
            pl.BlockSpec((S, RET_VAL_DIM), lambda b, h: (b, VR_BLK + h)),
            pl.BlockSpec((S, RET_VAL_DIM), lambda b, h: (b, GR_BLK + h)),
        ],
        out_specs=pl.BlockSpec((S, RET_VAL_DIM), lambda b, h: (b, h)),
        out_shape=jax.ShapeDtypeStruct((T, N_RET_HEADS * RET_VAL_DIM), BF16),
        compiler_params=pltpu.CompilerParams(
            dimension_semantics=("arbitrary", "arbitrary"), vmem_limit_bytes=VMEM_LIMIT),
        name="retention",
    )(cdec, dmat, qdec, kdec, g_ret_sub.reshape(N_RET_HEADS, 1, RET_VAL_DIM),
      proj, proj, proj, proj)


def _merge_kernel(x_ref, ya_ref, yr_ref, ga_ref, gr_ref, wbd_ref, wbr_ref, wo_ref, o_ref):
    a = jnp.dot(ya_ref[...], wbd_ref[...], preferred_element_type=F32)
    r = jnp.dot(yr_ref[...], wbr_ref[...], preferred_element_type=F32)
    mixed = (jax.nn.sigmoid(ga_ref[...].astype(F32)) * a
             + jax.nn.sigmoid(gr_ref[...].astype(F32)) * r)
    o_ref[...] = x_ref[...] + jnp.dot(mixed.astype(BF16), wo_ref[...],
                                      preferred_element_type=F32)


def _merge(x2d, ya, yr, proj, wbd, wbr, wo, tm=512):
    T = x2d.shape[0]
    row = lambda c: pl.BlockSpec((tm, D_MODEL), lambda i: (i, c))
    wspec = lambda: pl.BlockSpec((D_MODEL, D_MODEL), lambda i: (0, 0))
    return pl.pallas_call(
        _merge_kernel,
        grid=(T // tm,),
        in_specs=[row(0), row(0), row(0), row(GATE_A_BLK), row(GATE_R_BLK),
                  wspec(), wspec(), wspec()],
        out_specs=row(0),
        out_shape=jax.ShapeDtypeStruct((T, D_MODEL), F32),
        compiler_params=pltpu.CompilerParams(
            dimension_semantics=("arbitrary",), vmem_limit_bytes=VMEM_LIMIT),
        name="merge",
    )(x2d, ya, yr, proj, proj, wbd, wbr, wo)


def _mlp_kernel(x_ref, p_ref, gm_ref, w1_ref, w2_ref, gp_ref, wpg_ref, wple_ref, gf_ref, o_ref):
    x1 = x_ref[...]
    h2 = _rms(x1, gm_ref[...]).astype(BF16)
    acc = jnp.zeros_like(x1)
    for f in range(D_FF // D_MODEL):
        sl = slice(f * D_MODEL, (f + 1) * D_MODEL)
        a = jnp.maximum(jnp.dot(h2, w1_ref[:, sl], preferred_element_type=F32), 0.0)
        acc = acc + jnp.dot((a * a).astype(BF16), w2_ref[sl, :], preferred_element_type=F32)
    x2 = x1 + acc
    hp = _rms(x2, gp_ref[...]).astype(BF16)
    gate = jax.nn.sigmoid(jnp.dot(hp, wpg_ref[...], preferred_element_type=F32))
    pe = jnp.dot(p_ref[...].astype(BF16), wple_ref[...], preferred_element_type=F32)
    x3 = x2 + gate * pe
    o_ref[...] = _rms(x3, gf_ref[...])


def _mlp_ple(x1, p2d, g_mlp, w1, w2, g_ple, wpg, wple, g_final, tm=512):
    T = x1.shape[0]
    row = pl.BlockSpec((tm, D_MODEL), lambda i: (i, 0))
    vec = lambda: pl.BlockSpec((1, D_MODEL), lambda i: (0, 0))
    res = lambda shape: pl.BlockSpec(shape, lambda i: (0, 0), pipeline_mode=pl.Buffered(1))
    return pl.pallas_call(
        _mlp_kernel,
        grid=(T // tm,),
        in_specs=[row, pl.BlockSpec((tm, PLE_DIM), lambda i: (i, 0)), vec(),
                  res((D_MODEL, D_FF)), res((D_FF, D_MODEL)), vec(),
                  res((D_MODEL, D_MODEL)), res((PLE_DIM, D_MODEL)), vec()],
        out_specs=row,
        out_shape=jax.ShapeDtypeStruct((T, D_MODEL), F32),
        compiler_params=pltpu.CompilerParams(
            dimension_semantics=("arbitrary",), vmem_limit_bytes=VMEM_LIMIT),
        name="mlp_ple",
    )(x1, p2d, g_mlp, w1, w2, g_ple, wpg, wple, g_final)


def kernel(x, p, g_mix, w_in, lam_q1, lam_k1, lam_q2, lam_k2, g_diff_sub, g_ret_sub,
           w_branch_diff, w_branch_ret, w_out, g_mlp, w_ff1, w_ff2, g_ple, w_ple_gate,
           w_ple, g_final):
    B, S, D = x.shape
    T = B * S
    x2d = x.reshape(T, D)
    bf = lambda w: w.astype(BF16)

    proj = _in_proj(x2d, g_mix[0][None, :], bf(w_in[0]))
    ya = _diff_attn(proj, lam_q1[0][None, :], lam_k1[0][None, :],
                    lam_q2[0][None, :], lam_k2[0][None, :], g_diff_sub[0][None, :], B, S)
    yr = _retention(proj, g_ret_sub[0], B, S)
    x1 = _merge(x2d, ya, yr, proj, bf(w_branch_diff[0]), bf(w_branch_ret[0]), bf(w_out[0]))
    out = _mlp_ple(x1, p[0].reshape(T, PLE_DIM), g_mlp[0][None, :], bf(w_ff1[0]), bf(w_ff2[0]),
                   g_ple[0][None, :], bf(w_ple_gate[0]), bf(w_ple[0]), g_final[None, :])
    return out.reshape(B, S, D)
```

```python
import functools
import math

import jax
import jax.numpy as jnp
from jax import lax
from jax.experimental import pallas as pl
from jax.experimental.pallas import tpu as pltpu

F32 = jnp.float32
BF16 = jnp.bfloat16

D_MODEL = 1024
PLE_DIM = 256
N_DIFF_HEADS = 8
DIFF_HEAD_DIM = 64
DIFF_V_DIM = 128
N_RET_HEADS = 4
RET_KEY_DIM = 128
RET_VAL_DIM = 256
D_FF = 4096
D_IN = 8192
EPS = 1e-6
LAM_INIT = 0.8 - 0.6 * math.exp(-0.3 * 0)

ATT_COLS = 3 * 1024
QR_BLK, KR_BLK = 0, 1
VR_BLK, GR_BLK, GATE_A_BLK, GATE_R_BLK = 1, 2, 3, 4

ATT_T = 256
RET_C = 256
NEG = -1e30
LOG2E = math.log2(math.e)
ALIBI_PIECES = 3
VMEM_LIMIT = 56 * 1024 * 1024


def _rms(x, g):
    ms = jnp.mean(x * x, axis=-1, keepdims=True)
    return x * lax.rsqrt(ms + EPS) * g


def _in_proj_kernel(x_ref, g_ref, w_ref, att_ref, rest_ref, *, tn):
    h = _rms(x_ref[...], g_ref[...]).astype(BF16)
    hd = 2 * DIFF_HEAD_DIM
    for n in range(D_IN // tn):
        val = jnp.dot(h, w_ref[:, n * tn:(n + 1) * tn], preferred_element_type=F32).astype(BF16)
        if (n + 1) * tn <= ATT_COLS:
            for j in range(tn // hd):
                att_ref[0, n * (tn // hd) + j] = val[:, j * hd:(j + 1) * hd]
        else:
            rest_ref[:, n * tn - ATT_COLS:(n + 1) * tn - ATT_COLS] = val


def _in_proj(x2d, g_mix, w_in_bf16, B, S, tm=512, tn=1024):
    T = x2d.shape[0]
    hd = 2 * DIFF_HEAD_DIM
    spb = S // tm
    return pl.pallas_call(
        functools.partial(_in_proj_kernel, tn=tn),
        grid=(T // tm,),
        in_specs=[
            pl.BlockSpec((tm, D_MODEL), lambda i: (i, 0)),
            pl.BlockSpec((1, D_MODEL), lambda i: (0, 0)),
            pl.BlockSpec((D_MODEL, D_IN), lambda i: (0, 0), pipeline_mode=pl.Buffered(1)),
        ],
        out_specs=[
            pl.BlockSpec((1, ATT_COLS // hd, tm, hd), lambda i: (i // spb, 0, i % spb, 0)),
            pl.BlockSpec((tm, D_IN - ATT_COLS), lambda i: (i, 0)),
        ],
        out_shape=[jax.ShapeDtypeStruct((B, ATT_COLS // hd, S, hd), BF16),
                   jax.ShapeDtypeStruct((T, D_IN - ATT_COLS), BF16)],
        compiler_params=pltpu.CompilerParams(
            dimension_semantics=("arbitrary",), vmem_limit_bytes=VMEM_LIMIT),
        name="in_proj",
    )(x2d, g_mix, w_in_bf16)


def _attn_kernel(lq1_ref, lk1_ref, lq2_ref, lk2_ref, g_ref, kb_ref, att_ref, o_ref,
                 vT_scr, y_scr):
    T = ATT_T
    S = att_ref.shape[2]
    nq = S // T
    hd = 2 * DIFF_HEAD_DIM

    lam = (jnp.exp(jnp.sum(lq1_ref[...] * lk1_ref[...], axis=-1, keepdims=True))
           - jnp.exp(jnp.sum(lq2_ref[...] * lk2_ref[...], axis=-1, keepdims=True))
           + LAM_INIT)

    causal = (lax.broadcasted_iota(jnp.int32, (T, 2 * T), 0)
              <= (lax.broadcasted_iota(jnp.int32, (T, 2 * T), 1) & (T - 1)))
    lane = lax.broadcasted_iota(jnp.int32, (T, hd), 1)
    ones_cols = jnp.where(lane < ALIBI_PIECES, 1.0, 0.0).astype(BF16)
    nt = (((1,), (1,)), ((), ()))

    def head(h, carry):
        vT_scr[...] = att_ref[0, 2 * N_DIFF_HEADS + h].T

        def query_operand(qi):
            q = att_ref[0, h, qi * T:(qi + 1) * T, :].astype(F32)
            q = (q * (LOG2E * DIFF_HEAD_DIM ** -0.5)).astype(BF16)
            zero = jnp.zeros_like(q)
            qa = jnp.concatenate([jnp.where(lane < DIFF_HEAD_DIM, q, zero), ones_cols], axis=1)
            qb = jnp.concatenate([jnp.where(lane >= DIFF_HEAD_DIM, q, zero), ones_cols], axis=1)
            return jnp.concatenate([qa, qb], axis=0)

        def stage_a(qi):
            n = (qi + 1) * T
            k_aug = jnp.concatenate([att_ref[0, N_DIFF_HEADS + h, 0:n, :], kb_ref[h, 0:n, :]],
                                    axis=1)
            u = lax.dot_general(k_aug, query_operand(qi), nt, preferred_element_type=F32)
            tiles = [u[j * T:(j + 1) * T, :] for j in range(qi)]
            tiles.append(jnp.where(causal, u[qi * T:, :], NEG))
            m = jnp.max(tiles[-1], axis=0, keepdims=True)
            if qi > 0:
                m = jnp.maximum(m, jnp.max(u[:qi * T, :], axis=0, keepdims=True))
            return tiles, m

        def stage_b(qi, tiles, m):
            l = jnp.zeros((1, 2 * T), F32)
            acc = jnp.zeros((DIFF_V_DIM, 2 * T), F32)
            for j, u in enumerate(tiles):
                p = jnp.exp2(u - m)
                l = l + jnp.sum(p, axis=0, keepdims=True)
                acc = acc + jnp.dot(vT_scr[:, j * T:(j + 1) * T], p.astype(BF16),
                                    preferred_element_type=F32)
            r = acc * (1.0 / l)
            o = r[:, :T] - lam * r[:, T:]
            ms = jnp.mean(o * o, axis=0, keepdims=True)
            y = o * (lax.rsqrt(ms + EPS) * (1.0 - LAM_INIT))
            y_scr[h, qi * T:(qi + 1) * T, :] = (y.T * g_ref[...]).astype(y_scr.dtype)

        pend = []
        for qi in range(nq):
            pend.append((qi, stage_a(qi)))
            if len(pend) > 2:
                j, (tiles, m) = pend.pop(0)
                stage_b(j, tiles, m)
        for j, (tiles, m) in pend:
            stage_b(j, tiles, m)
        return carry

    lax.fori_loop(0, N_DIFF_HEADS, head, 0)
    for h in range(N_DIFF_HEADS):
        o_ref[:, h * DIFF_V_DIM:(h + 1) * DIFF_V_DIM] = y_scr[h]


def _alibi_key_table(S):
    slopes = 2.0 ** (-8.0 * jnp.arange(1, N_DIFF_HEADS + 1, dtype=F32) / N_DIFF_HEADS)
    rem = (LOG2E * slopes)[:, None] * jnp.arange(S, dtype=F32)[None, :]
    pieces = []
    for _ in range(ALIBI_PIECES):
        piece = lax.bitcast_convert_type(
            lax.bitcast_convert_type(rem, jnp.uint32) & jnp.uint32(0xFFFF0000), F32)
        pieces.append(piece.astype(BF16))
        rem = rem - piece
    tab = jnp.stack(pieces, axis=-1)
    return jnp.pad(tab, ((0, 0), (0, 0), (0, 2 * DIFF_HEAD_DIM - ALIBI_PIECES)))


def _diff_attn(att, lq1, lk1, lq2, lk2, g_sub):
    B, G, S, hd = att.shape
    small = lambda: pl.BlockSpec((1, DIFF_HEAD_DIM), lambda b: (0, 0))
    return pl.pallas_call(
        _attn_kernel,
        grid=(B,),
        in_specs=[
            small(), small(), small(), small(),
            pl.BlockSpec((1, DIFF_V_DIM), lambda b: (0, 0)),
            pl.BlockSpec((N_DIFF_HEADS, S, hd), lambda b: (0, 0, 0), pipeline_mode=pl.Buffered(1)),
            pl.BlockSpec((1, G, S, hd), lambda b: (b, 0, 0, 0)),
        ],
        out_specs=pl.BlockSpec((S, N_DIFF_HEADS * DIFF_V_DIM), lambda b: (b, 0)),
        out_shape=jax.ShapeDtypeStruct((B * S, N_DIFF_HEADS * DIFF_V_DIM), BF16),
        scratch_shapes=[
            pltpu.VMEM((DIFF_V_DIM, S), BF16),
            pltpu.VMEM((N_DIFF_HEADS, S, DIFF_V_DIM), BF16),
        ],
        compiler_params=pltpu.CompilerParams(
            dimension_semantics=("arbitrary",), vmem_limit_bytes=VMEM_LIMIT),
        name="diff_attn",
    )(lq1, lk1, lq2, lk2, g_sub, _alibi_key_table(S), att)


def _ret_kernel(cdec_ref, dmat_ref, qdec_ref, kdec_ref, g_ref,
                q_ref, k_ref, v_ref, gr_ref, o_ref):
    C = RET_C
    S = q_ref.shape[0]
    nt = (((1,), (1,)), ((), ()))
    for h in range(N_RET_HEADS):
        kcol = slice(h * RET_KEY_DIM, (h + 1) * RET_KEY_DIM)
        vcol = slice(h * RET_VAL_DIM, (h + 1) * RET_VAL_DIM)
        cd = cdec_ref[h]
        r = jnp.zeros((RET_KEY_DIM, RET_VAL_DIM), F32)
        for i in range(S // C):
            rows = slice(i * C, (i + 1) * C)
            qi = q_ref[rows, kcol]
            ki = k_ref[rows, kcol]
            vi = v_ref[rows, vcol]
            s = lax.dot_general(qi, ki, nt, preferred_element_type=F32) * dmat_ref[h]
            o = jnp.dot(s.astype(BF16), vi, preferred_element_type=F32)
            if i > 0:
                o = o + jnp.dot(qi, r.astype(BF16), preferred_element_type=F32) * qdec_ref[h]
            if i + 1 < S // C:
                kd = (ki.astype(F32) * kdec_ref[h]).astype(BF16)
                r = r * cd + jnp.dot(kd.T, vi, preferred_element_type=F32)
            y = _rms(o, g_ref[h])
            gr = gr_ref[rows, vcol].astype(F32)
            o_ref[rows, vcol] = (y * (gr * jax.nn.sigmoid(gr))).astype(o_ref.dtype)


def _retention(rest, g_ret_sub, B, S):
    T = B * S
    C = RET_C
    H = N_RET_HEADS
    hh = jnp.arange(H, dtype=F32)
    log_gamma = jnp.log(1.0 - 2.0 ** (-5.0 - hh))
    pos = jnp.arange(C, dtype=F32)
    rel = pos[:, None] - pos[None, :]
    kscale = RET_KEY_DIM ** -0.5
    dmat = jnp.where(rel >= 0, jnp.exp(log_gamma[:, None, None] * rel), 0.0) * kscale
    qdec = jnp.broadcast_to(jnp.exp(log_gamma[:, None] * (pos[None, :] + 1.0))[:, :, None],
                            (H, C, RET_VAL_DIM))
    kdec = jnp.broadcast_to(
        (jnp.exp(log_gamma[:, None] * (C - 1.0 - pos[None, :])) * kscale)[:, :, None],
        (H, C, RET_KEY_DIM))
    cdec = jnp.exp(log_gamma * C)
    whole = lambda shape: pl.BlockSpec(shape, lambda b: (0,) * len(shape))
    return pl.pallas_call(
        _ret_kernel,
        grid=(B,),
        in_specs=[
            pl.BlockSpec(memory_space=pltpu.SMEM),
            whole((H, C, C)), whole((H, C, RET_VAL_DIM)), whole((H, C, RET_KEY_DIM)),
            whole((H, 1, RET_VAL_DIM)),
            pl.BlockSpec((S, H * RET_KEY_DIM), lambda b: (b, QR_BLK)),
            pl.BlockSpec((S, H * RET_KEY_DIM), lambda b: (b, KR_BLK)),
            pl.BlockSpec((S, H * RET_VAL_DIM), lambda b: (b, VR_BLK)),
            pl.BlockSpec((S, H * RET_VAL_DIM), lambda b: (b, GR_BLK)),
        ],
        out_specs=pl.BlockSpec((S, H * RET_VAL_DIM), lambda b: (b, 0)),
        out_shape=jax.ShapeDtypeStruct((T, H * RET_VAL_DIM), BF16),
        compiler_params=pltpu.CompilerParams(
            dimension_semantics=("arbitrary",), vmem_limit_bytes=VMEM_LIMIT),
        name="retention",
    )(cdec, dmat, qdec, kdec, g_ret_sub.reshape(H, 1, RET_VAL_DIM), rest, rest, rest, rest)


def _merge_kernel(x_ref, ya_ref, yr_ref, ga_ref, gr_ref, wbd_ref, wbr_ref, wo_ref, o_ref,
                  *, n_sub):
    ts = x_ref.shape[0] // n_sub

    def mix(s):
        rows = slice(s * ts, (s + 1) * ts)
        a = jnp.dot(ya_ref[rows, :], wbd_ref[...], preferred_element_type=F32)
        r = jnp.dot(yr_ref[rows, :], wbr_ref[...], preferred_element_type=F32)
        return (jax.nn.sigmoid(ga_ref[rows, :].astype(F32)) * a
                + jax.nn.sigmoid(gr_ref[rows, :].astype(F32)) * r).astype(BF16)

    def project(s, mixed):
        rows = slice(s * ts, (s + 1) * ts)
        o_ref[rows, :] = x_ref[rows, :] + jnp.dot(mixed, wo_ref[...], preferred_element_type=F32)

    prev = None
    for s in range(n_sub):
        cur = mix(s)
        if prev is not None:
            project(s - 1, prev)
        prev = cur
    project(n_sub - 1, prev)


def _merge(x2d, ya, yr, rest, wbd, wbr, wo, tm=1024, n_sub=2):
    T = x2d.shape[0]
    row = lambda c: pl.BlockSpec((tm, D_MODEL), lambda i: (i, c))
    wspec = lambda: pl.BlockSpec((D_MODEL, D_MODEL), lambda i: (0, 0))
    return pl.pallas_call(
        functools.partial(_merge_kernel, n_sub=n_sub),
        grid=(T // tm,),
        in_specs=[row(0), row(0), row(0), row(GATE_A_BLK), row(GATE_R_BLK),
                  wspec(), wspec(), wspec()],
        out_specs=row(0),
        out_shape=jax.ShapeDtypeStruct((T, D_MODEL), F32),
        compiler_params=pltpu.CompilerParams(
            dimension_semantics=("arbitrary",), vmem_limit_bytes=VMEM_LIMIT),
        name="merge",
    )(x2d, ya, yr, rest, rest, wbd, wbr, wo)


def _mlp_kernel(x_ref, p_ref, gm_ref, w1_ref, w2_ref, gp_ref, wpg_ref, wple_ref, gf_ref, o_ref,
                *, n_sub):
    ts = x_ref.shape[0] // n_sub

    def ffn(s):
        x1 = x_ref[s * ts:(s + 1) * ts, :]
        h2 = _rms(x1, gm_ref[...]).astype(BF16)
        acc = jnp.zeros_like(x1)
        for f in range(D_FF // D_MODEL):
            sl = slice(f * D_MODEL, (f + 1) * D_MODEL)
            a = jnp.maximum(jnp.dot(h2, w1_ref[:, sl], preferred_element_type=F32), 0.0)
            acc = acc + jnp.dot((a * a).astype(BF16), w2_ref[sl, :], preferred_element_type=F32)
        return x1 + acc

    def ple_and_norm(s, x2):
        rows = slice(s * ts, (s + 1) * ts)
        hp = _rms(x2, gp_ref[...]).astype(BF16)
        gate = jax.nn.sigmoid(jnp.dot(hp, wpg_ref[...], preferred_element_type=F32))
        pe = jnp.dot(p_ref[rows, :].astype(BF16), wple_ref[...], preferred_element_type=F32)
        o_ref[rows, :] = _rms(x2 + gate * pe, gf_ref[...])

    prev = None
    for s in range(n_sub):
        cur = ffn(s)
        if prev is not None:
            ple_and_norm(s - 1, prev)
        prev = cur
    ple_and_norm(n_sub - 1, prev)


def _mlp_ple(x1, p2d, g_mlp, w1, w2, g_ple, wpg, wple, g_final, tm=1024, n_sub=4):
    T = x1.shape[0]
    row = pl.BlockSpec((tm, D_MODEL), lambda i: (i, 0))
    vec = lambda: pl.BlockSpec((1, D_MODEL), lambda i: (0, 0))
    res = lambda shape: pl.BlockSpec(shape, lambda i: (0, 0), pipeline_mode=pl.Buffered(1))
    return pl.pallas_call(
        functools.partial(_mlp_kernel, n_sub=n_sub),
        grid=(T // tm,),
        in_specs=[row, pl.BlockSpec((tm, PLE_DIM), lambda i: (i, 0)), vec(),
                  res((D_MODEL, D_FF)), res((D_FF, D_MODEL)), vec(),
                  res((D_MODEL, D_MODEL)), res((PLE_DIM, D_MODEL)), vec()],
        out_specs=row,
        out_shape=jax.ShapeDtypeStruct((T, D_MODEL), F32),
        compiler_params=pltpu.CompilerParams(
            dimension_semantics=("arbitrary",), vmem_limit_bytes=VMEM_LIMIT),
        name="mlp_ple",
    )(x1, p2d, g_mlp, w1, w2, g_ple, wpg, wple, g_final)


def kernel(x, p, g_mix, w_in, lam_q1, lam_k1, lam_q2, lam_k2, g_diff_sub, g_ret_sub,
           w_branch_diff, w_branch_ret, w_out, g_mlp, w_ff1, w_ff2, g_ple, w_ple_gate,
           w_ple, g_final):
    B, S, D = x.shape
    T = B * S
    x2d = x.reshape(T, D)
    bf = lambda w: w.astype(BF16)

    att, rest = _in_proj(x2d, g_mix[0][None, :], bf(w_in[0]), B, S)
    ya = _diff_attn(att, lam_q1[0][None, :], lam_k1[0][None, :],
                    lam_q2[0][None, :], lam_k2[0][None, :], g_diff_sub[0][None, :])
    yr = _retention(rest, g_ret_sub[0], B, S)
    x1 = _merge(x2d, ya, yr, rest, bf(w_branch_diff[0]), bf(w_branch_ret[0]), bf(w_out[0]))
    out = _mlp_ple(x1, p[0].reshape(T, PLE_DIM), g_mlp[0][None, :], bf(w_ff1[0]), bf(w_ff2[0]),
                   g_ple[0][None, :], bf(w_ple_gate[0]), bf(w_ple[0]), g_final[None, :])
    return out.reshape(B, S, D)
```

```python
import functools
import math

import jax
import jax.numpy as jnp
from jax import lax
from jax.experimental import pallas as pl
from jax.experimental.pallas import tpu as pltpu

F32 = jnp.float32
BF16 = jnp.bfloat16

D_MODEL = 1024
PLE_DIM = 256
N_DIFF_HEADS = 8
DIFF_HEAD_DIM = 64
DIFF_V_DIM = 128
N_RET_HEADS = 4
RET_KEY_DIM = 128
RET_VAL_DIM = 256
D_FF = 4096
D_IN = 8192
EPS = 1e-6
LAM_INIT = 0.8 - 0.6 * math.exp(-0.3 * 0)

ATT_COLS = 3 * 1024
QR_BLK, KR_BLK = 0, 1
VR_BLK, GR_BLK, GATE_A_BLK, GATE_R_BLK = 1, 2, 3, 4

ATT_T = 256
RET_C = 256
NEG = -1e30
LOG2E = math.log2(math.e)
ONES_ROWS = 16
ALIBI_PIECES = 3
VMEM_LIMIT = 56 * 1024 * 1024


def _rms(x, g):
    ms = jnp.mean(x * x, axis=-1, keepdims=True)
    return x * lax.rsqrt(ms + EPS) * g


def _in_proj_kernel(x_ref, g_ref, w_ref, att_ref, rest_ref, *, tn):
    h = _rms(x_ref[...], g_ref[...]).astype(BF16)
    hd = 2 * DIFF_HEAD_DIM
    for n in range(D_IN // tn):
        val = jnp.dot(h, w_ref[:, n * tn:(n + 1) * tn], preferred_element_type=F32).astype(BF16)
        if (n + 1) * tn <= ATT_COLS:
            for j in range(tn // hd):
                att_ref[0, n * (tn // hd) + j] = val[:, j * hd:(j + 1) * hd]
        else:
            rest_ref[:, n * tn - ATT_COLS:(n + 1) * tn - ATT_COLS] = val


def _in_proj(x2d, g_mix, w_in_bf16, B, S, tm=512, tn=1024):
    T = x2d.shape[0]
    hd = 2 * DIFF_HEAD_DIM
    spb = S // tm
    return pl.pallas_call(
        functools.partial(_in_proj_kernel, tn=tn),
        grid=(T // tm,),
        in_specs=[
            pl.BlockSpec((tm, D_MODEL), lambda i: (i, 0)),
            pl.BlockSpec((1, D_MODEL), lambda i: (0, 0)),
            pl.BlockSpec((D_MODEL, D_IN), lambda i: (0, 0), pipeline_mode=pl.Buffered(1)),
        ],
        out_specs=[
            pl.BlockSpec((1, ATT_COLS // hd, tm, hd), lambda i: (i // spb, 0, i % spb, 0)),
            pl.BlockSpec((tm, D_IN - ATT_COLS), lambda i: (i, 0)),
        ],
        out_shape=[jax.ShapeDtypeStruct((B, ATT_COLS // hd, S, hd), BF16),
                   jax.ShapeDtypeStruct((T, D_IN - ATT_COLS), BF16)],
        compiler_params=pltpu.CompilerParams(
            dimension_semantics=("arbitrary",), vmem_limit_bytes=VMEM_LIMIT),
        name="in_proj",
    )(x2d, g_mix, w_in_bf16)


def _attn_kernel(lq1_ref, lk1_ref, lq2_ref, lk2_ref, g_ref, kb_ref, att_ref, o_ref,
                 vT_scr, y_scr):
    T = ATT_T
    S = att_ref.shape[2]
    nq = S // T
    hd = 2 * DIFF_HEAD_DIM

    lam = (jnp.exp(jnp.sum(lq1_ref[...] * lk1_ref[...], axis=-1, keepdims=True))
           - jnp.exp(jnp.sum(lq2_ref[...] * lk2_ref[...], axis=-1, keepdims=True))
           + LAM_INIT)

    causal = (lax.broadcasted_iota(jnp.int32, (T, 2 * T), 0)
              <= (lax.broadcasted_iota(jnp.int32, (T, 2 * T), 1) & (T - 1)))
    lane = lax.broadcasted_iota(jnp.int32, (T, hd), 1)
    ones_cols = jnp.where(lane < ALIBI_PIECES, 1.0, 0.0).astype(BF16)
    nt = (((1,), (1,)), ((), ()))

    vT_scr[DIFF_V_DIM:, :] = jnp.ones((ONES_ROWS, S), BF16)

    def head(h, carry):
        vT_scr[0:DIFF_V_DIM, :] = att_ref[0, 2 * N_DIFF_HEADS + h].T

        def query_operand(qi):
            q = att_ref[0, h, qi * T:(qi + 1) * T, :].astype(F32)
            q = (q * (LOG2E * DIFF_HEAD_DIM ** -0.5)).astype(BF16)
            zero = jnp.zeros_like(q)
            qa = jnp.concatenate([jnp.where(lane < DIFF_HEAD_DIM, q, zero), ones_cols], axis=1)
            qb = jnp.concatenate([jnp.where(lane >= DIFF_HEAD_DIM, q, zero), ones_cols], axis=1)
            return jnp.concatenate([qa, qb], axis=0)

        def stage_a(qi):
            n = (qi + 1) * T
            k_aug = jnp.concatenate([att_ref[0, N_DIFF_HEADS + h, 0:n, :], kb_ref[h, 0:n, :]],
                                    axis=1)
            u = lax.dot_general(k_aug, query_operand(qi), nt, preferred_element_type=F32)
            tiles = [u[j * T:(j + 1) * T, :] for j in range(qi)]
            tiles.append(jnp.where(causal, u[qi * T:, :], NEG))
            m = jnp.max(tiles[-1], axis=0, keepdims=True)
            if qi > 0:
                m = jnp.maximum(m, jnp.max(u[:qi * T, :], axis=0, keepdims=True))
            return tiles, m

        def stage_b(qi, tiles, m):
            acc = jnp.zeros((DIFF_V_DIM + ONES_ROWS, 2 * T), F32)
            for j, u in enumerate(tiles):
                p = jnp.exp2(u - m).astype(BF16)
                acc = acc + jnp.dot(vT_scr[:, j * T:(j + 1) * T], p,
                                    preferred_element_type=F32)
            r = acc[0:DIFF_V_DIM] * (1.0 / acc[DIFF_V_DIM:DIFF_V_DIM + 1])
            o = r[:, :T] - lam * r[:, T:]
            ms = jnp.mean(o * o, axis=0, keepdims=True)
            y = o * (lax.rsqrt(ms + EPS) * (1.0 - LAM_INIT))
            y_scr[h, qi * T:(qi + 1) * T, :] = (y.T * g_ref[...]).astype(y_scr.dtype)

        pend = []
        for qi in range(nq):
            pend.append((qi, stage_a(qi)))
            if len(pend) > 2:
                j, (tiles, m) = pend.pop(0)
                stage_b(j, tiles, m)
        for j, (tiles, m) in pend:
            stage_b(j, tiles, m)
        return carry

    lax.fori_loop(0, N_DIFF_HEADS, head, 0)
    for h in range(N_DIFF_HEADS):
        o_ref[:, h * DIFF_V_DIM:(h + 1) * DIFF_V_DIM] = y_scr[h]


def _alibi_key_table(S):
    slopes = 2.0 ** (-8.0 * jnp.arange(1, N_DIFF_HEADS + 1, dtype=F32) / N_DIFF_HEADS)
    rem = (LOG2E * slopes)[:, None] * jnp.arange(S, dtype=F32)[None, :]
    pieces = []
    for _ in range(ALIBI_PIECES):
        piece = lax.bitcast_convert_type(
            lax.bitcast_convert_type(rem, jnp.uint32) & jnp.uint32(0xFFFF0000), F32)
        pieces.append(piece.astype(BF16))
        rem = rem - piece
    tab = jnp.stack(pieces, axis=-1)
    return jnp.pad(tab, ((0, 0), (0, 0), (0, 2 * DIFF_HEAD_DIM - ALIBI_PIECES)))


def _diff_attn(att, lq1, lk1, lq2, lk2, g_sub):
    B, G, S, hd = att.shape
    small = lambda: pl.BlockSpec((1, DIFF_HEAD_DIM), lambda b: (0, 0))
    return pl.pallas_call(
        _attn_kernel,
        grid=(B,),
        in_specs=[
            small(), small(), small(), small(),
            pl.BlockSpec((1, DIFF_V_DIM), lambda b: (0, 0)),
            pl.BlockSpec((N_DIFF_HEADS, S, hd), lambda b: (0, 0, 0), pipeline_mode=pl.Buffered(1)),
            pl.BlockSpec((1, G, S, hd), lambda b: (b, 0, 0, 0)),
        ],
        out_specs=pl.BlockSpec((S, N_DIFF_HEADS * DIFF_V_DIM), lambda b: (b, 0)),
        out_shape=jax.ShapeDtypeStruct((B * S, N_DIFF_HEADS * DIFF_V_DIM), BF16),
        scratch_shapes=[
            pltpu.VMEM((DIFF_V_DIM + ONES_ROWS, S), BF16),
            pltpu.VMEM((N_DIFF_HEADS, S, DIFF_V_DIM), BF16),
        ],
        compiler_params=pltpu.CompilerParams(
            dimension_semantics=("arbitrary",), vmem_limit_bytes=VMEM_LIMIT),
        name="diff_attn",
    )(lq1, lk1, lq2, lk2, g_sub, _alibi_key_table(S), att)


def _ret_kernel(cdec_ref, dmat_ref, qdec_ref, kdec_ref, g_ref,
                q_ref, k_ref, v_ref, gr_ref, o_ref):
    C = RET_C
    S = q_ref.shape[0]
    nt = (((1,), (1,)), ((), ()))
    for h in range(N_RET_HEADS):
        kcol = slice(h * RET_KEY_DIM, (h + 1) * RET_KEY_DIM)
        vcol = slice(h * RET_VAL_DIM, (h + 1) * RET_VAL_DIM)
        cd = cdec_ref[h]
        r = jnp.zeros((RET_KEY_DIM, RET_VAL_DIM), F32)
        for i in range(S // C):
            rows = slice(i * C, (i + 1) * C)
            qi = q_ref[rows, kcol]
            ki = k_ref[rows, kcol]
            vi = v_ref[rows, vcol]
            s = lax.dot_general(qi, ki, nt, preferred_element_type=F32) * dmat_ref[h]
            o = jnp.dot(s.astype(BF16), vi, preferred_element_type=F32)
            if i > 0:
                o = o + jnp.dot(qi, r.astype(BF16), preferred_element_type=F32) * qdec_ref[h]
            if i + 1 < S // C:
                kd = (ki.astype(F32) * kdec_ref[h]).astype(BF16)
                r = r * cd + jnp.dot(kd.T, vi, preferred_element_type=F32)
            y = _rms(o, g_ref[h])
            gr = gr_ref[rows, vcol].astype(F32)
            o_ref[rows, vcol] = (y * (gr * jax.nn.sigmoid(gr))).astype(o_ref.dtype)


def _retention(rest, g_ret_sub, B, S):
    T = B * S
    C = RET_C
    H = N_RET_HEADS
    hh = jnp.arange(H, dtype=F32)
    log_gamma = jnp.log(1.0 - 2.0 ** (-5.0 - hh))
    pos = jnp.arange(C, dtype=F32)
    rel = pos[:, None] - pos[None, :]
    kscale = RET_KEY_DIM ** -0.5
    dmat = jnp.where(rel >= 0, jnp.exp(log_gamma[:, None, None] * rel), 0.0) * kscale
    qdec = jnp.broadcast_to(jnp.exp(log_gamma[:, None] * (pos[None, :] + 1.0))[:, :, None],
                            (H, C, RET_VAL_DIM))
    kdec = jnp.broadcast_to(
        (jnp.exp(log_gamma[:, None] * (C - 1.0 - pos[None, :])) * kscale)[:, :, None],
        (H, C, RET_KEY_DIM))
    cdec = jnp.exp(log_gamma * C)
    whole = lambda shape: pl.BlockSpec(shape, lambda b: (0,) * len(shape))
    return pl.pallas_call(
        _ret_kernel,
        grid=(B,),
        in_specs=[
            pl.BlockSpec(memory_space=pltpu.SMEM),
            whole((H, C, C)), whole((H, C, RET_VAL_DIM)), whole((H, C, RET_KEY_DIM)),
            whole((H, 1, RET_VAL_DIM)),
            pl.BlockSpec((S, H * RET_KEY_DIM), lambda b: (b, QR_BLK)),
            pl.BlockSpec((S, H * RET_KEY_DIM), lambda b: (b, KR_BLK)),
            pl.BlockSpec((S, H * RET_VAL_DIM), lambda b: (b, VR_BLK)),
            pl.BlockSpec((S, H * RET_VAL_DIM), lambda b: (b, GR_BLK)),
        ],
        out_specs=pl.BlockSpec((S, H * RET_VAL_DIM), lambda b: (b, 0)),
        out_shape=jax.ShapeDtypeStruct((T, H * RET_VAL_DIM), BF16),
        compiler_params=pltpu.CompilerParams(
            dimension_semantics=("arbitrary",), vmem_limit_bytes=VMEM_LIMIT),
        name="retention",
    )(cdec, dmat, qdec, kdec, g_ret_sub.reshape(H, 1, RET_VAL_DIM), rest, rest, rest, rest)


def _merge_kernel(x_ref, ya_ref, yr_ref, ga_ref, gr_ref, wbd_ref, wbr_ref, wo_ref, o_ref,
                  *, n_sub):
    ts = x_ref.shape[0] // n_sub

    def mix(s):
        rows = slice(s * ts, (s + 1) * ts)
        a = jnp.dot(ya_ref[rows, :], wbd_ref[...], preferred_element_type=F32)
        r = jnp.dot(yr_ref[rows, :], wbr_ref[...], preferred_element_type=F32)
        return (jax.nn.sigmoid(ga_ref[rows, :].astype(F32)) * a
                + jax.nn.sigmoid(gr_ref[rows, :].astype(F32)) * r).astype(BF16)

    def project(s, mixed):
        rows = slice(s * ts, (s + 1) * ts)
        o_ref[rows, :] = x_ref[rows, :] + jnp.dot(mixed, wo_ref[...], preferred_element_type=F32)

    prev = None
    for s in range(n_sub):
        cur = mix(s)
        if prev is not None:
            project(s - 1, prev)
        prev = cur
    project(n_sub - 1, prev)


def _merge(x2d, ya, yr, rest, wbd, wbr, wo, tm=1024, n_sub=2):
    T = x2d.shape[0]
    row = lambda c: pl.BlockSpec((tm, D_MODEL), lambda i: (i, c))
    wspec = lambda: pl.BlockSpec((D_MODEL, D_MODEL), lambda i: (0, 0))
    return pl.pallas_call(
        functools.partial(_merge_kernel, n_sub=n_sub),
        grid=(T // tm,),
        in_specs=[row(0), row(0), row(0), row(GATE_A_BLK), row(GATE_R_BLK),
                  wspec(), wspec(), wspec()],
        out_specs=row(0),
        out_shape=jax.ShapeDtypeStruct((T, D_MODEL), F32),
        compiler_params=pltpu.CompilerParams(
            dimension_semantics=("arbitrary",), vmem_limit_bytes=VMEM_LIMIT),
        name="merge",
    )(x2d, ya, yr, rest, rest, wbd, wbr, wo)


def _mlp_kernel(x_ref, p_ref, gm_ref, w1_ref, w2_ref, gp_ref, wpg_ref, wple_ref, gf_ref, o_ref,
                *, n_sub):
    ts = x_ref.shape[0] // n_sub

    def ffn(s):
        x1 = x_ref[s * ts:(s + 1) * ts, :]
        h2 = _rms(x1, gm_ref[...]).astype(BF16)
        acc = jnp.zeros_like(x1)
        for f in range(D_FF // D_MODEL):
            sl = slice(f * D_MODEL, (f + 1) * D_MODEL)
            a = jnp.maximum(jnp.dot(h2, w1_ref[:, sl], preferred_element_type=F32), 0.0)
            acc = acc + jnp.dot((a * a).astype(BF16), w2_ref[sl, :], preferred_element_type=F32)
        return x1 + acc

    def ple_and_norm(s, x2):
        rows = slice(s * ts, (s + 1) * ts)
        hp = _rms(x2, gp_ref[...]).astype(BF16)
        gate = jax.nn.sigmoid(jnp.dot(hp, wpg_ref[...], preferred_element_type=F32))
        pe = jnp.dot(p_ref[rows, :].astype(BF16), wple_ref[...], preferred_element_type=F32)
        o_ref[rows, :] = _rms(x2 + gate * pe, gf_ref[...])

    prev = None
    for s in range(n_sub):
        cur = ffn(s)
        if prev is not None:
            ple_and_norm(s - 1, prev)
        prev = cur
    ple_and_norm(n_sub - 1, prev)


def _mlp_ple(x1, p2d, g_mlp, w1, w2, g_ple, wpg, wple, g_final, tm=1024, n_sub=4):
    T = x1.shape[0]
    row = pl.BlockSpec((tm, D_MODEL), lambda i: (i, 0))
    vec = lambda: pl.BlockSpec((1, D_MODEL), lambda i: (0, 0))
    res = lambda shape: pl.BlockSpec(shape, lambda i: (0, 0), pipeline_mode=pl.Buffered(1))
    return pl.pallas_call(
        functools.partial(_mlp_kernel, n_sub=n_sub),
        grid=(T // tm,),
        in_specs=[row, pl.BlockSpec((tm, PLE_DIM), lambda i: (i, 0)), vec(),
                  res((D_MODEL, D_FF)), res((D_FF, D_MODEL)), vec(),
                  res((D_MODEL, D_MODEL)), res((PLE_DIM, D_MODEL)), vec()],
        out_specs=row,
        out_shape=jax.ShapeDtypeStruct((T, D_MODEL), F32),
        compiler_params=pltpu.CompilerParams(
            dimension_semantics=("arbitrary",), vmem_limit_bytes=VMEM_LIMIT),
        name="mlp_ple",
    )(x1, p2d, g_mlp, w1, w2, g_ple, wpg, wple, g_final)


def kernel(x, p, g_mix, w_in, lam_q1, lam_k1, lam_q2, lam_k2, g_diff_sub, g_ret_sub,
           w_branch_diff, w_branch_ret, w_out, g_mlp, w_ff1, w_ff2, g_ple, w_ple_gate,
           w_ple, g_final):
    B, S, D = x.shape
    T = B * S
    x2d = x.reshape(T, D)
    bf = lambda w: w.astype(BF16)

    att, rest = _in_proj(x2d, g_mix[0][None, :], bf(w_in[0]), B, S)
    ya = _diff_attn(att, lam_q1[0][None, :], lam_k1[0][None, :],
                    lam_q2[0][None, :], lam_k2[0][None, :], g_diff_sub[0][None, :])
    yr = _retention(rest, g_ret_sub[0], B, S)
    x1 = _merge(x2d, ya, yr, rest, bf(w_branch_diff[0]), bf(w_branch_ret[0]), bf(w_out[0]))
    out = _mlp_ple(x1, p[0].reshape(T, PLE_DIM), g_mlp[0][None, :], bf(w_ff1[0]), bf(w_ff2[0]),
                   g_ple[0][None, :], bf(w_ple_gate[0]), bf(w_ple[0]), g_final[None, :])
    return out.reshape(B, S, D)
```

```python
import functools
import math

import jax
import jax.numpy as jnp
from jax import lax
from jax.experimental import pallas as pl
from jax.experimental.pallas import tpu as pltpu

F32 = jnp.float32
BF16 = jnp.bfloat16

D_MODEL = 1024
PLE_DIM = 256
N_DIFF_HEADS = 8
DIFF_HEAD_DIM = 64
DIFF_V_DIM = 128
N_RET_HEADS = 4
RET_KEY_DIM = 128
RET_VAL_DIM = 256
D_FF = 4096
D_IN = 8192
EPS = 1e-6
LAM_INIT = 0.8 - 0.6 * math.exp(-0.3 * 0)

QK_COLS = 2 * 1024
ATT_COLS = 3 * 1024
QR_BLK, KR_BLK = 0, 1
VR_BLK, GR_BLK, GATE_A_BLK, GATE_R_BLK = 1, 2, 3, 4

ATT_T = 256
RET_C = 256
NEG = -1e30
LOG2E = math.log2(math.e)
ONES_ROWS = 16
ALIBI_PIECES = 3
VMEM_LIMIT = 56 * 1024 * 1024


def _rms(x, g):
    ms = jnp.mean(x * x, axis=-1, keepdims=True)
    return x * lax.rsqrt(ms + EPS) * g


def _in_proj_kernel(x_ref, g_ref, w_ref, qk_ref, vT_ref, rest_ref, *, tn):
    h = _rms(x_ref[...], g_ref[...]).astype(BF16)
    hd = 2 * DIFF_HEAD_DIM
    per = tn // hd
    tm = x_ref.shape[0]
    vT_ref[0, :, DIFF_V_DIM:, :] = jnp.ones((N_DIFF_HEADS, ONES_ROWS, tm), BF16)
    for n in range(D_IN // tn):
        val = jnp.dot(h, w_ref[:, n * tn:(n + 1) * tn], preferred_element_type=F32).astype(BF16)
        if (n + 1) * tn <= QK_COLS:
            for j in range(per):
                qk_ref[0, n * per + j] = val[:, j * hd:(j + 1) * hd]
        elif (n + 1) * tn <= ATT_COLS:
            for j in range(per):
                vT_ref[0, (n * tn - QK_COLS) // hd + j, 0:DIFF_V_DIM, :] = (
                    val[:, j * hd:(j + 1) * hd].T)
        else:
            rest_ref[:, n * tn - ATT_COLS:(n + 1) * tn - ATT_COLS] = val


def _in_proj(x2d, g_mix, w_in_bf16, B, S, tm=512, tn=1024):
    T = x2d.shape[0]
    hd = 2 * DIFF_HEAD_DIM
    spb = S // tm
    return pl.pallas_call(
        functools.partial(_in_proj_kernel, tn=tn),
        grid=(T // tm,),
        in_specs=[
            pl.BlockSpec((tm, D_MODEL), lambda i: (i, 0)),
            pl.BlockSpec((1, D_MODEL), lambda i: (0, 0)),
            pl.BlockSpec((D_MODEL, D_IN), lambda i: (0, 0), pipeline_mode=pl.Buffered(1)),
        ],
        out_specs=[
            pl.BlockSpec((1, QK_COLS // hd, tm, hd), lambda i: (i // spb, 0, i % spb, 0)),
            pl.BlockSpec((1, N_DIFF_HEADS, DIFF_V_DIM + ONES_ROWS, tm),
                         lambda i: (i // spb, 0, 0, i % spb)),
            pl.BlockSpec((tm, D_IN - ATT_COLS), lambda i: (i, 0)),
        ],
        out_shape=[jax.ShapeDtypeStruct((B, QK_COLS // hd, S, hd), BF16),
                   jax.ShapeDtypeStruct((B, N_DIFF_HEADS, DIFF_V_DIM + ONES_ROWS, S), BF16),
                   jax.ShapeDtypeStruct((T, D_IN - ATT_COLS), BF16)],
        compiler_params=pltpu.CompilerParams(
            dimension_semantics=("arbitrary",), vmem_limit_bytes=VMEM_LIMIT),
        name="in_proj",
    )(x2d, g_mix, w_in_bf16)


def _attn_kernel(lq1_ref, lk1_ref, lq2_ref, lk2_ref, gcol_ref, kb_ref, att_ref, vT_ref, o_ref,
                 y_scr):
    T = ATT_T
    S = att_ref.shape[2]
    nq = S // T
    hd = 2 * DIFF_HEAD_DIM

    lam = (jnp.exp(jnp.sum(lq1_ref[...] * lk1_ref[...], axis=-1, keepdims=True))
           - jnp.exp(jnp.sum(lq2_ref[...] * lk2_ref[...], axis=-1, keepdims=True))
           + LAM_INIT)

    causal = (lax.broadcasted_iota(jnp.int32, (T, 2 * T), 0)
              <= (lax.broadcasted_iota(jnp.int32, (T, 2 * T), 1) & (T - 1)))
    lane = lax.broadcasted_iota(jnp.int32, (T, hd), 1)
    ones_cols = jnp.where(lane < ALIBI_PIECES, 1.0, 0.0).astype(BF16)
    nt = (((1,), (1,)), ((), ()))

    g_col = jnp.broadcast_to(gcol_ref[...], (DIFF_V_DIM, T))

    def head(h, carry):
        def query_operand(qi):
            q = att_ref[0, h, qi * T:(qi + 1) * T, :].astype(F32)
            q = (q * (LOG2E * DIFF_HEAD_DIM ** -0.5)).astype(BF16)
            zero = jnp.zeros_like(q)
            qa = jnp.concatenate([jnp.where(lane < DIFF_HEAD_DIM, q, zero), ones_cols], axis=1)
            qb = jnp.concatenate([jnp.where(lane >= DIFF_HEAD_DIM, q, zero), ones_cols], axis=1)
            return jnp.concatenate([qa, qb], axis=0)

        def stage_a(qi):
            n = (qi + 1) * T
            k_aug = jnp.concatenate([att_ref[0, N_DIFF_HEADS + h, 0:n, :], kb_ref[h, 0:n, :]],
                                    axis=1)
            u = lax.dot_general(k_aug, query_operand(qi), nt, preferred_element_type=F32)
            tiles = [u[j * T:(j + 1) * T, :] for j in range(qi)]
            tiles.append(jnp.where(causal, u[qi * T:, :], NEG))
            m = jnp.max(tiles[-1], axis=0, keepdims=True)
            if qi > 0:
                m = jnp.maximum(m, jnp.max(u[:qi * T, :], axis=0, keepdims=True))
            return tiles, m

        def stage_b(qi, tiles, m):
            acc = jnp.zeros((DIFF_V_DIM + ONES_ROWS, 2 * T), F32)
            for j, u in enumerate(tiles):
                p = jnp.exp2(u - m).astype(BF16)
                acc = acc + jnp.dot(vT_ref[0, h, :, j * T:(j + 1) * T], p,
                                    preferred_element_type=F32)
            r = acc[0:DIFF_V_DIM] * (1.0 / acc[DIFF_V_DIM:DIFF_V_DIM + 1])
            o = r[:, :T] - lam * r[:, T:]
            ms = jnp.mean(o * o, axis=0, keepdims=True)
            y = (o * (lax.rsqrt(ms + EPS) * (1.0 - LAM_INIT)) * g_col).astype(y_scr.dtype)
            y_scr[h, qi * T:(qi + 1) * T, :] = y.T

        pend = []
        for qi in range(nq):
            pend.append((qi, stage_a(qi)))
            if len(pend) > 2:
                j, (tiles, m) = pend.pop(0)
                stage_b(j, tiles, m)
        for j, (tiles, m) in pend:
            stage_b(j, tiles, m)
        return carry

    lax.fori_loop(0, N_DIFF_HEADS, head, 0)
    for h in range(N_DIFF_HEADS):
        o_ref[:, h * DIFF_V_DIM:(h + 1) * DIFF_V_DIM] = y_scr[h]


def _alibi_key_table(S):
    slopes = 2.0 ** (-8.0 * jnp.arange(1, N_DIFF_HEADS + 1, dtype=F32) / N_DIFF_HEADS)
    rem = (LOG2E * slopes)[:, None] * jnp.arange(S, dtype=F32)[None, :]
    pieces = []
    for _ in range(ALIBI_PIECES):
        piece = lax.bitcast_convert_type(
            lax.bitcast_convert_type(rem, jnp.uint32) & jnp.uint32(0xFFFF0000), F32)
        pieces.append(piece.astype(BF16))
        rem = rem - piece
    tab = jnp.stack(pieces, axis=-1)
    return jnp.pad(tab, ((0, 0), (0, 0), (0, 2 * DIFF_HEAD_DIM - ALIBI_PIECES)))


def _diff_attn(qk, vT, lq1, lk1, lq2, lk2, g_sub):
    B, G, S, hd = qk.shape
    small = lambda: pl.BlockSpec((1, DIFF_HEAD_DIM), lambda b: (0, 0))
    return pl.pallas_call(
        _attn_kernel,
        grid=(B,),
        in_specs=[
            small(), small(), small(), small(),
            pl.BlockSpec((DIFF_V_DIM, 1), lambda b: (0, 0)),
            pl.BlockSpec((N_DIFF_HEADS, S, hd), lambda b: (0, 0, 0), pipeline_mode=pl.Buffered(1)),
            pl.BlockSpec((1, G, S, hd), lambda b: (b, 0, 0, 0)),
            pl.BlockSpec((1, N_DIFF_HEADS, DIFF_V_DIM + ONES_ROWS, S), lambda b: (b, 0, 0, 0)),
        ],
        out_specs=pl.BlockSpec((S, N_DIFF_HEADS * DIFF_V_DIM), lambda b: (b, 0)),
        out_shape=jax.ShapeDtypeStruct((B * S, N_DIFF_HEADS * DIFF_V_DIM), BF16),
        scratch_shapes=[
            pltpu.VMEM((N_DIFF_HEADS, S, DIFF_V_DIM), BF16),
        ],
        compiler_params=pltpu.CompilerParams(
            dimension_semantics=("arbitrary",), vmem_limit_bytes=VMEM_LIMIT),
        name="diff_attn",
    )(lq1, lk1, lq2, lk2, g_sub, _alibi_key_table(S), qk, vT)


def _ret_kernel(cdec_ref, dmat_ref, qdec_ref, kdec_ref, g_ref,
                q_ref, k_ref, v_ref, gr_ref, o_ref):
    C = RET_C
    S = q_ref.shape[0]
    nt = (((1,), (1,)), ((), ()))
    for h in range(N_RET_HEADS):
        kcol = slice(h * RET_KEY_DIM, (h + 1) * RET_KEY_DIM)
        vcol = slice(h * RET_VAL_DIM, (h + 1) * RET_VAL_DIM)
        cd = cdec_ref[h]
        r = jnp.zeros((RET_KEY_DIM, RET_VAL_DIM), F32)
        for i in range(S // C):
            rows = slice(i * C, (i + 1) * C)
            qi = q_ref[rows, kcol]
            ki = k_ref[rows, kcol]
            vi = v_ref[rows, vcol]
            s = lax.dot_general(qi, ki, nt, preferred_element_type=F32) * dmat_ref[h]
            o = jnp.dot(s.astype(BF16), vi, preferred_element_type=F32)
            if i > 0:
                o = o + jnp.dot(qi, r.astype(BF16), preferred_element_type=F32) * qdec_ref[h]
            if i + 1 < S // C:
                kd = (ki.astype(F32) * kdec_ref[h]).astype(BF16)
                r = r * cd + jnp.dot(kd.T, vi, preferred_element_type=F32)
            y = _rms(o, g_ref[h])
            gr = gr_ref[rows, vcol].astype(F32)
            o_ref[rows, vcol] = (y * (gr * jax.nn.sigmoid(gr))).astype(o_ref.dtype)


def _retention(rest, g_ret_sub, B, S):
    T = B * S
    C = RET_C
    H = N_RET_HEADS
    hh = jnp.arange(H, dtype=F32)
    log_gamma = jnp.log(1.0 - 2.0 ** (-5.0 - hh))
    pos = jnp.arange(C, dtype=F32)
    rel = pos[:, None] - pos[None, :]
    kscale = RET_KEY_DIM ** -0.5
    dmat = jnp.where(rel >= 0, jnp.exp(log_gamma[:, None, None] * rel), 0.0) * kscale
    qdec = jnp.broadcast_to(jnp.exp(log_gamma[:, None] * (pos[None, :] + 1.0))[:, :, None],
                            (H, C, RET_VAL_DIM))
    kdec = jnp.broadcast_to(
        (jnp.exp(log_gamma[:, None] * (C - 1.0 - pos[None, :])) * kscale)[:, :, None],
        (H, C, RET_KEY_DIM))
    cdec = jnp.exp(log_gamma * C)
    whole = lambda shape: pl.BlockSpec(shape, lambda b: (0,) * len(shape))
    return pl.pallas_call(
        _ret_kernel,
        grid=(B,),
        in_specs=[
            pl.BlockSpec(memory_space=pltpu.SMEM),
            whole((H, C, C)), whole((H, C, RET_VAL_DIM)), whole((H, C, RET_KEY_DIM)),
            whole((H, 1, RET_VAL_DIM)),
            pl.BlockSpec((S, H * RET_KEY_DIM), lambda b: (b, QR_BLK)),
            pl.BlockSpec((S, H * RET_KEY_DIM), lambda b: (b, KR_BLK)),
            pl.BlockSpec((S, H * RET_VAL_DIM), lambda b: (b, VR_BLK)),
            pl.BlockSpec((S, H * RET_VAL_DIM), lambda b: (b, GR_BLK)),
        ],
        out_specs=pl.BlockSpec((S, H * RET_VAL_DIM), lambda b: (b, 0)),
        out_shape=jax.ShapeDtypeStruct((T, H * RET_VAL_DIM), BF16),
        compiler_params=pltpu.CompilerParams(
            dimension_semantics=("arbitrary",), vmem_limit_bytes=VMEM_LIMIT),
        name="retention",
    )(cdec, dmat, qdec, kdec, g_ret_sub.reshape(H, 1, RET_VAL_DIM), rest, rest, rest, rest)


def _merge_kernel(x_ref, ya_ref, yr_ref, ga_ref, gr_ref, wbd_ref, wbr_ref, wo_ref, o_ref,
                  *, n_sub):
    ts = x_ref.shape[0] // n_sub

    def mix(s):
        rows = slice(s * ts, (s + 1) * ts)
        a = jnp.dot(ya_ref[rows, :], wbd_ref[...], preferred_element_type=F32)
        r = jnp.dot(yr_ref[rows, :], wbr_ref[...], preferred_element_type=F32)
        return (jax.nn.sigmoid(ga_ref[rows, :].astype(F32)) * a
                + jax.nn.sigmoid(gr_ref[rows, :].astype(F32)) * r).astype(BF16)

    def project(s, mixed):
        rows = slice(s * ts, (s + 1) * ts)
        o_ref[rows, :] = x_ref[rows, :] + jnp.dot(mixed, wo_ref[...], preferred_element_type=F32)

    prev = None
    for s in range(n_sub):
        cur = mix(s)
        if prev is not None:
            project(s - 1, prev)
        prev = cur
    project(n_sub - 1, prev)


def _merge(x2d, ya, yr, rest, wbd, wbr, wo, tm=1024, n_sub=2):
    T = x2d.shape[0]
    row = lambda c: pl.BlockSpec((tm, D_MODEL), lambda i: (i, c))
    wspec = lambda: pl.BlockSpec((D_MODEL, D_MODEL), lambda i: (0, 0))
    return pl.pallas_call(
        functools.partial(_merge_kernel, n_sub=n_sub),
        grid=(T // tm,),
        in_specs=[row(0), row(0), row(0), row(GATE_A_BLK), row(GATE_R_BLK),
                  wspec(), wspec(), wspec()],
        out_specs=row(0),
        out_shape=jax.ShapeDtypeStruct((T, D_MODEL), F32),
        compiler_params=pltpu.CompilerParams(
            dimension_semantics=("arbitrary",), vmem_limit_bytes=VMEM_LIMIT),
        name="merge",
    )(x2d, ya, yr, rest, rest, wbd, wbr, wo)


def _mlp_kernel(x_ref, p_ref, gm_ref, w1_ref, w2_ref, gp_ref, wpg_ref, wple_ref, gf_ref, o_ref,
                *, n_sub):
    ts = x_ref.shape[0] // n_sub

    def ffn(s):
        x1 = x_ref[s * ts:(s + 1) * ts, :]
        h2 = _rms(x1, gm_ref[...]).astype(BF16)
        acc = jnp.zeros_like(x1)
        for f in range(D_FF // D_MODEL):
            sl = slice(f * D_MODEL, (f + 1) * D_MODEL)
            a = jnp.maximum(jnp.dot(h2, w1_ref[:, sl], preferred_element_type=F32), 0.0)
            acc = acc + jnp.dot((a * a).astype(BF16), w2_ref[sl, :], preferred_element_type=F32)
        return x1 + acc

    def ple_and_norm(s, x2):
        rows = slice(s * ts, (s + 1) * ts)
        hp = _rms(x2, gp_ref[...]).astype(BF16)
        gate = jax.nn.sigmoid(jnp.dot(hp, wpg_ref[...], preferred_element_type=F32))
        pe = jnp.dot(p_ref[rows, :].astype(BF16), wple_ref[...], preferred_element_type=F32)
        o_ref[rows, :] = _rms(x2 + gate * pe, gf_ref[...])

    prev = None
    for s in range(n_sub):
        cur = ffn(s)
        if prev is not None:
            ple_and_norm(s - 1, prev)
        prev = cur
    ple_and_norm(n_sub - 1, prev)


def _mlp_ple(x1, p2d, g_mlp, w1, w2, g_ple, wpg, wple, g_final, tm=1024, n_sub=4):
    T = x1.shape[0]
    row = pl.BlockSpec((tm, D_MODEL), lambda i: (i, 0))
    vec = lambda: pl.BlockSpec((1, D_MODEL), lambda i: (0, 0))
    res = lambda shape: pl.BlockSpec(shape, lambda i: (0, 0), pipeline_mode=pl.Buffered(1))
    return pl.pallas_call(
        functools.partial(_mlp_kernel, n_sub=n_sub),
        grid=(T // tm,),
        in_specs=[row, pl.BlockSpec((tm, PLE_DIM), lambda i: (i, 0)), vec(),
                  res((D_MODEL, D_FF)), res((D_FF, D_MODEL)), vec(),
                  res((D_MODEL, D_MODEL)), res((PLE_DIM, D_MODEL)), vec()],
        out_specs=row,
        out_shape=jax.ShapeDtypeStruct((T, D_MODEL), F32),
        compiler_params=pltpu.CompilerParams(
            dimension_semantics=("arbitrary",), vmem_limit_bytes=VMEM_LIMIT),
        name="mlp_ple",
    )(x1, p2d, g_mlp, w1, w2, g_ple, wpg, wple, g_final)


def kernel(x, p, g_mix, w_in, lam_q1, lam_k1, lam_q2, lam_k2, g_diff_sub, g_ret_sub,
           w_branch_diff, w_branch_ret, w_out, g_mlp, w_ff1, w_ff2, g_ple, w_ple_gate,
           w_ple, g_final):
    B, S, D = x.shape
    T = B * S
    x2d = x.reshape(T, D)
    bf = lambda w: w.astype(BF16)

    qk, vT, rest = _in_proj(x2d, g_mix[0][None, :], bf(w_in[0]), B, S)
    ya = _diff_attn(qk, vT, lam_q1[0][None, :], lam_k1[0][None, :],
                    lam_q2[0][None, :], lam_k2[0][None, :], g_diff_sub[0][:, None])
    yr = _retention(rest, g_ret_sub[0], B, S)
    x1 = _merge(x2d, ya, yr, rest, bf(w_branch_diff[0]), bf(w_branch_ret[0]), bf(w_out[0]))
    out = _mlp_ple(x1, p[0].reshape(T, PLE_DIM), g_mlp[0][None, :], bf(w_ff1[0]), bf(w_ff2[0]),
                   g_ple[0][None, :], bf(w_ple_gate[0]), bf(w_ple[0]), g_final[None, :])
    return out.reshape(B, S, D)
```

```python
import functools
import math

import jax
import jax.numpy as jnp
from jax import lax
from jax.experimental import pallas as pl
from jax.experimental.pallas import tpu as pltpu

F32 = jnp.float32
BF16 = jnp.bfloat16

D_MODEL = 1024
PLE_DIM = 256
N_DIFF_HEADS = 8
DIFF_HEAD_DIM = 64
DIFF_V_DIM = 128
N_RET_HEADS = 4
RET_KEY_DIM = 128
RET_VAL_DIM = 256
D_FF = 4096
D_IN = 8192
EPS = 1e-6
LAM_INIT = 0.8 - 0.6 * math.exp(-0.3 * 0)

QK_COLS = 2 * 1024
ATT_COLS = 3 * 1024
QR_BLK, KR_BLK = 0, 1
VR_BLK, GR_BLK, GATE_A_BLK, GATE_R_BLK = 1, 2, 3, 4

ATT_T = 256
RET_C = 256
NEG = -1e30
LOG2E = math.log2(math.e)
ONES_ROWS = 16
ALIBI_PIECES = 3
VMEM_LIMIT = 56 * 1024 * 1024


def _sigmoid(x):
    return 0.5 * jnp.tanh(0.5 * x) + 0.5


def _rms(x, g):
    ms = jnp.mean(x * x, axis=-1, keepdims=True)
    return x * lax.rsqrt(ms + EPS) * g


def _in_proj_kernel(x_ref, g_ref, w_ref, qk_ref, vT_ref, rest_ref, *, tn):
    h = _rms(x_ref[...], g_ref[...]).astype(BF16)
    hd = 2 * DIFF_HEAD_DIM
    per = tn // hd
    tm = x_ref.shape[0]
    vT_ref[0, :, DIFF_V_DIM:, :] = jnp.ones((N_DIFF_HEADS, ONES_ROWS, tm), BF16)
    for n in range(D_IN // tn):
        val = jnp.dot(h, w_ref[:, n * tn:(n + 1) * tn], preferred_element_type=F32).astype(BF16)
        if (n + 1) * tn <= QK_COLS:
            for j in range(per):
                qk_ref[0, n * per + j] = val[:, j * hd:(j + 1) * hd]
        elif (n + 1) * tn <= ATT_COLS:
            for j in range(per):
                vT_ref[0, (n * tn - QK_COLS) // hd + j, 0:DIFF_V_DIM, :] = (
                    val[:, j * hd:(j + 1) * hd].T)
        else:
            rest_ref[:, n * tn - ATT_COLS:(n + 1) * tn - ATT_COLS] = val


def _in_proj(x2d, g_mix, w_in_bf16, B, S, tm=512, tn=1024):
    T = x2d.shape[0]
    hd = 2 * DIFF_HEAD_DIM
    spb = S // tm
    return pl.pallas_call(
        functools.partial(_in_proj_kernel, tn=tn),
        grid=(T // tm,),
        in_specs=[
            pl.BlockSpec((tm, D_MODEL), lambda i: (i, 0)),
            pl.BlockSpec((1, D_MODEL), lambda i: (0, 0)),
            pl.BlockSpec((D_MODEL, D_IN), lambda i: (0, 0), pipeline_mode=pl.Buffered(1)),
        ],
        out_specs=[
            pl.BlockSpec((1, QK_COLS // hd, tm, hd), lambda i: (i // spb, 0, i % spb, 0)),
            pl.BlockSpec((1, N_DIFF_HEADS, DIFF_V_DIM + ONES_ROWS, tm),
                         lambda i: (i // spb, 0, 0, i % spb)),
            pl.BlockSpec((tm, D_IN - ATT_COLS), lambda i: (i, 0)),
        ],
        out_shape=[jax.ShapeDtypeStruct((B, QK_COLS // hd, S, hd), BF16),
                   jax.ShapeDtypeStruct((B, N_DIFF_HEADS, DIFF_V_DIM + ONES_ROWS, S), BF16),
                   jax.ShapeDtypeStruct((T, D_IN - ATT_COLS), BF16)],
        compiler_params=pltpu.CompilerParams(
            dimension_semantics=("arbitrary",), vmem_limit_bytes=VMEM_LIMIT),
        name="in_proj",
    )(x2d, g_mix, w_in_bf16)


def _attn_kernel(lq1_ref, lk1_ref, lq2_ref, lk2_ref, gcol_ref, kb_ref, att_ref, vT_ref, o_ref,
                 y_scr):
    T = ATT_T
    S = att_ref.shape[2]
    nq = S // T
    hd = 2 * DIFF_HEAD_DIM

    lam = (jnp.exp(jnp.sum(lq1_ref[...] * lk1_ref[...], axis=-1, keepdims=True))
           - jnp.exp(jnp.sum(lq2_ref[...] * lk2_ref[...], axis=-1, keepdims=True))
           + LAM_INIT)

    causal = (lax.broadcasted_iota(jnp.int32, (T, 2 * T), 0)
              <= (lax.broadcasted_iota(jnp.int32, (T, 2 * T), 1) & (T - 1)))
    lane = lax.broadcasted_iota(jnp.int32, (T, hd), 1)
    ones_cols = jnp.where(lane < ALIBI_PIECES, 1.0, 0.0).astype(BF16)
    nt = (((1,), (1,)), ((), ()))

    g_col = jnp.broadcast_to(gcol_ref[...], (DIFF_V_DIM, T))

    def head(h, carry):
        def query_operand(qi):
            q = att_ref[0, h, qi * T:(qi + 1) * T, :].astype(F32)
            q = (q * (LOG2E * DIFF_HEAD_DIM ** -0.5)).astype(BF16)
            zero = jnp.zeros_like(q)
            qa = jnp.concatenate([jnp.where(lane < DIFF_HEAD_DIM, q, zero), ones_cols], axis=1)
            qb = jnp.concatenate([jnp.where(lane >= DIFF_HEAD_DIM, q, zero), ones_cols], axis=1)
            return jnp.concatenate([qa, qb], axis=0)

        def stage_a(qi):
            n = (qi + 1) * T
            k_aug = jnp.concatenate([att_ref[0, N_DIFF_HEADS + h, 0:n, :], kb_ref[h, 0:n, :]],
                                    axis=1)
            u = lax.dot_general(k_aug, query_operand(qi), nt, preferred_element_type=F32)
            tiles = [u[j * T:(j + 1) * T, :] for j in range(qi)]
            tiles.append(jnp.where(causal, u[qi * T:, :], NEG))
            m = jnp.max(tiles[-1], axis=0, keepdims=True)
            if qi > 0:
                m = jnp.maximum(m, jnp.max(u[:qi * T, :], axis=0, keepdims=True))
            return tiles, m

        def stage_b(qi, tiles, m):
            acc = jnp.zeros((DIFF_V_DIM + ONES_ROWS, 2 * T), F32)
            for j, u in enumerate(tiles):
                p = jnp.exp2(u - m).astype(BF16)
                acc = acc + jnp.dot(vT_ref[0, h, :, j * T:(j + 1) * T], p,
                                    preferred_element_type=F32)
            r = acc[0:DIFF_V_DIM] * (1.0 / acc[DIFF_V_DIM:DIFF_V_DIM + 1])
            o = r[:, :T] - lam * r[:, T:]
            ms = jnp.mean(o * o, axis=0, keepdims=True)
            y = (o * (lax.rsqrt(ms + EPS) * (1.0 - LAM_INIT)) * g_col).astype(y_scr.dtype)
            y_scr[h, qi * T:(qi + 1) * T, :] = y.T

        pend = []
        for qi in range(nq):
            pend.append((qi, stage_a(qi)))
            if len(pend) > 2:
                j, (tiles, m) = pend.pop(0)
                stage_b(j, tiles, m)
        for j, (tiles, m) in pend:
            stage_b(j, tiles, m)
        return carry

    lax.fori_loop(0, N_DIFF_HEADS, head, 0)
    for h in range(N_DIFF_HEADS):
        o_ref[:, h * DIFF_V_DIM:(h + 1) * DIFF_V_DIM] = y_scr[h]


def _alibi_key_table(S):
    slopes = 2.0 ** (-8.0 * jnp.arange(1, N_DIFF_HEADS + 1, dtype=F32) / N_DIFF_HEADS)
    rem = (LOG2E * slopes)[:, None] * jnp.arange(S, dtype=F32)[None, :]
    pieces = []
    for _ in range(ALIBI_PIECES):
        piece = lax.bitcast_convert_type(
            lax.bitcast_convert_type(rem, jnp.uint32) & jnp.uint32(0xFFFF0000), F32)
        pieces.append(piece.astype(BF16))
        rem = rem - piece
    tab = jnp.stack(pieces, axis=-1)
    return jnp.pad(tab, ((0, 0), (0, 0), (0, 2 * DIFF_HEAD_DIM - ALIBI_PIECES)))


def _diff_attn(qk, vT, lq1, lk1, lq2, lk2, g_sub):
    B, G, S, hd = qk.shape
    small = lambda: pl.BlockSpec((1, DIFF_HEAD_DIM), lambda b: (0, 0))
    return pl.pallas_call(
        _attn_kernel,
        grid=(B,),
        in_specs=[
            small(), small(), small(), small(),
            pl.BlockSpec((DIFF_V_DIM, 1), lambda b: (0, 0)),
            pl.BlockSpec((N_DIFF_HEADS, S, hd), lambda b: (0, 0, 0), pipeline_mode=pl.Buffered(1)),
            pl.BlockSpec((1, G, S, hd), lambda b: (b, 0, 0, 0)),
            pl.BlockSpec((1, N_DIFF_HEADS, DIFF_V_DIM + ONES_ROWS, S), lambda b: (b, 0, 0, 0)),
        ],
        out_specs=pl.BlockSpec((S, N_DIFF_HEADS * DIFF_V_DIM), lambda b: (b, 0)),
        out_shape=jax.ShapeDtypeStruct((B * S, N_DIFF_HEADS * DIFF_V_DIM), BF16),
        scratch_shapes=[
            pltpu.VMEM((N_DIFF_HEADS, S, DIFF_V_DIM), BF16),
        ],
        compiler_params=pltpu.CompilerParams(
            dimension_semantics=("arbitrary",), vmem_limit_bytes=VMEM_LIMIT),
        name="diff_attn",
    )(lq1, lk1, lq2, lk2, g_sub, _alibi_key_table(S), qk, vT)


def _ret_kernel(cdec_ref, dmat_ref, qdec_ref, kdec_ref, g_ref,
                q_ref, k_ref, v_ref, gr_ref, o_ref):
    C = RET_C
    S = q_ref.shape[0]
    nt = (((1,), (1,)), ((), ()))
    for h in range(N_RET_HEADS):
        kcol = slice(h * RET_KEY_DIM, (h + 1) * RET_KEY_DIM)
        vcol = slice(h * RET_VAL_DIM, (h + 1) * RET_VAL_DIM)
        cd = cdec_ref[h]
        r = jnp.zeros((RET_KEY_DIM, RET_VAL_DIM), F32)
        for i in range(S // C):
            rows = slice(i * C, (i + 1) * C)
            qi = q_ref[rows, kcol]
            ki = k_ref[rows, kcol]
            vi = v_ref[rows, vcol]
            s = lax.dot_general(qi, ki, nt, preferred_element_type=F32) * dmat_ref[h]
            o = jnp.dot(s.astype(BF16), vi, preferred_element_type=F32)
            if i > 0:
                o = o + jnp.dot(qi, r.astype(BF16), preferred_element_type=F32) * qdec_ref[h]
            if i + 1 < S // C:
                kd = (ki.astype(F32) * kdec_ref[h]).astype(BF16)
                r = r * cd + jnp.dot(kd.T, vi, preferred_element_type=F32)
            y = _rms(o, g_ref[h])
            gr = gr_ref[rows, vcol].astype(F32)
            o_ref[rows, vcol] = (y * (gr * _sigmoid(gr))).astype(o_ref.dtype)


def _retention(rest, g_ret_sub, B, S):
    T = B * S
    C = RET_C
    H = N_RET_HEADS
    hh = jnp.arange(H, dtype=F32)
    log_gamma = jnp.log(1.0 - 2.0 ** (-5.0 - hh))
    pos = jnp.arange(C, dtype=F32)
    rel = pos[:, None] - pos[None, :]
    kscale = RET_KEY_DIM ** -0.5
    dmat = jnp.where(rel >= 0, jnp.exp(log_gamma[:, None, None] * rel), 0.0) * kscale
    qdec = jnp.broadcast_to(jnp.exp(log_gamma[:, None] * (pos[None, :] + 1.0))[:, :, None],
                            (H, C, RET_VAL_DIM))
    kdec = jnp.broadcast_to(
        (jnp.exp(log_gamma[:, None] * (C - 1.0 - pos[None, :])) * kscale)[:, :, None],
        (H, C, RET_KEY_DIM))
    cdec = jnp.exp(log_gamma * C)
    whole = lambda shape: pl.BlockSpec(shape, lambda b: (0,) * len(shape))
    return pl.pallas_call(
        _ret_kernel,
        grid=(B,),
        in_specs=[
            pl.BlockSpec(memory_space=pltpu.SMEM),
            whole((H, C, C)), whole((H, C, RET_VAL_DIM)), whole((H, C, RET_KEY_DIM)),
            whole((H, 1, RET_VAL_DIM)),
            pl.BlockSpec((S, H * RET_KEY_DIM), lambda b: (b, QR_BLK)),
            pl.BlockSpec((S, H * RET_KEY_DIM), lambda b: (b, KR_BLK)),
            pl.BlockSpec((S, H * RET_VAL_DIM), lambda b: (b, VR_BLK)),
            pl.BlockSpec((S, H * RET_VAL_DIM), lambda b: (b, GR_BLK)),
        ],
        out_specs=pl.BlockSpec((S, H * RET_VAL_DIM), lambda b: (b, 0)),
        out_shape=jax.ShapeDtypeStruct((T, H * RET_VAL_DIM), BF16),
        compiler_params=pltpu.CompilerParams(
            dimension_semantics=("arbitrary",), vmem_limit_bytes=VMEM_LIMIT),
        name="retention",
    )(cdec, dmat, qdec, kdec, g_ret_sub.reshape(H, 1, RET_VAL_DIM), rest, rest, rest, rest)


def _merge_kernel(x_ref, ya_ref, yr_ref, ga_ref, gr_ref, wbd_ref, wbr_ref, wo_ref, o_ref,
                  *, n_sub):
    ts = x_ref.shape[0] // n_sub

    def mix(s):
        rows = slice(s * ts, (s + 1) * ts)
        a = jnp.dot(ya_ref[rows, :], wbd_ref[...], preferred_element_type=F32)
        r = jnp.dot(yr_ref[rows, :], wbr_ref[...], preferred_element_type=F32)
        return (_sigmoid(ga_ref[rows, :].astype(F32)) * a
                + _sigmoid(gr_ref[rows, :].astype(F32)) * r).astype(BF16)

    def project(s, mixed):
        rows = slice(s * ts, (s + 1) * ts)
        o_ref[rows, :] = x_ref[rows, :] + jnp.dot(mixed, wo_ref[...], preferred_element_type=F32)

    prev = None
    for s in range(n_sub):
        cur = mix(s)
        if prev is not None:
            project(s - 1, prev)
        prev = cur
    project(n_sub - 1, prev)


def _merge(x2d, ya, yr, rest, wbd, wbr, wo, tm=1024, n_sub=2):
    T = x2d.shape[0]
    row = lambda c: pl.BlockSpec((tm, D_MODEL), lambda i: (i, c))
    wspec = lambda: pl.BlockSpec((D_MODEL, D_MODEL), lambda i: (0, 0))
    return pl.pallas_call(
        functools.partial(_merge_kernel, n_sub=n_sub),
        grid=(T // tm,),
        in_specs=[row(0), row(0), row(0), row(GATE_A_BLK), row(GATE_R_BLK),
                  wspec(), wspec(), wspec()],
        out_specs=row(0),
        out_shape=jax.ShapeDtypeStruct((T, D_MODEL), F32),
        compiler_params=pltpu.CompilerParams(
            dimension_semantics=("arbitrary",), vmem_limit_bytes=VMEM_LIMIT),
        name="merge",
    )(x2d, ya, yr, rest, rest, wbd, wbr, wo)


def _mlp_kernel(x_ref, p_ref, gm_ref, w1_ref, w2_ref, gp_ref, wpg_ref, wple_ref, gf_ref, o_ref,
                *, n_sub):
    ts = x_ref.shape[0] // n_sub

    def ffn(s):
        x1 = x_ref[s * ts:(s + 1) * ts, :]
        h2 = _rms(x1, gm_ref[...]).astype(BF16)
        acc = jnp.zeros_like(x1)
        for f in range(D_FF // D_MODEL):
            sl = slice(f * D_MODEL, (f + 1) * D_MODEL)
            a = jnp.maximum(jnp.dot(h2, w1_ref[:, sl], preferred_element_type=F32), 0.0)
            acc = acc + jnp.dot((a * a).astype(BF16), w2_ref[sl, :], preferred_element_type=F32)
        return x1 + acc

    def ple_and_norm(s, x2):
        rows = slice(s * ts, (s + 1) * ts)
        hp = _rms(x2, gp_ref[...]).astype(BF16)
        gate = _sigmoid(jnp.dot(hp, wpg_ref[...], preferred_element_type=F32))
        pe = jnp.dot(p_ref[rows, :].astype(BF16), wple_ref[...], preferred_element_type=F32)
        o_ref[rows, :] = _rms(x2 + gate * pe, gf_ref[...])

    prev = None
    for s in range(n_sub):
        cur = ffn(s)
        if prev is not None:
            ple_and_norm(s - 1, prev)
        prev = cur
    ple_and_norm(n_sub - 1, prev)


def _mlp_ple(x1, p2d, g_mlp, w1, w2, g_ple, wpg, wple, g_final, tm=1024, n_sub=4):
    T = x1.shape[0]
    row = pl.BlockSpec((tm, D_MODEL), lambda i: (i, 0))
    vec = lambda: pl.BlockSpec((1, D_MODEL), lambda i: (0, 0))
    res = lambda shape: pl.BlockSpec(shape, lambda i: (0, 0), pipeline_mode=pl.Buffered(1))
    return pl.pallas_call(
        functools.partial(_mlp_kernel, n_sub=n_sub),
        grid=(T // tm,),
        in_specs=[row, pl.BlockSpec((tm, PLE_DIM), lambda i: (i, 0)), vec(),
                  res((D_MODEL, D_FF)), res((D_FF, D_MODEL)), vec(),
                  res((D_MODEL, D_MODEL)), res((PLE_DIM, D_MODEL)), vec()],
        out_specs=row,
        out_shape=jax.ShapeDtypeStruct((T, D_MODEL), F32),
        compiler_params=pltpu.CompilerParams(
            dimension_semantics=("arbitrary",), vmem_limit_bytes=VMEM_LIMIT),
        name="mlp_ple",
    )(x1, p2d, g_mlp, w1, w2, g_ple, wpg, wple, g_final)


def kernel(x, p, g_mix, w_in, lam_q1, lam_k1, lam_q2, lam_k2, g_diff_sub, g_ret_sub,
           w_branch_diff, w_branch_ret, w_out, g_mlp, w_ff1, w_ff2, g_ple, w_ple_gate,
           w_ple, g_final):
    B, S, D = x.shape
    T = B * S
    x2d = x.reshape(T, D)
    bf = lambda w: w.astype(BF16)

    qk, vT, rest = _in_proj(x2d, g_mix[0][None, :], bf(w_in[0]), B, S)
    ya = _diff_attn(qk, vT, lam_q1[0][None, :], lam_k1[0][None, :],
                    lam_q2[0][None, :], lam_k2[0][None, :], g_diff_sub[0][:, None])
    yr = _retention(rest, g_ret_sub[0], B, S)
    x1 = _merge(x2d, ya, yr, rest, bf(w_branch_diff[0]), bf(w_branch_ret[0]), bf(w_out[0]))
    out = _mlp_ple(x1, p[0].reshape(T, PLE_DIM), g_mlp[0][None, :], bf(w_ff1[0]), bf(w_ff2[0]),
                   g_ple[0][None, :], bf(w_ple_gate[0]), bf(w_ple[0]), g_final[None, :])
    return out.reshape(B, S, D)
```

```python
import functools
import math

import jax
import jax.numpy as jnp
from jax import lax
from jax.experimental import pallas as pl
from jax.experimental.pallas import tpu as pltpu

F32 = jnp.float32
BF16 = jnp.bfloat16

D_MODEL = 1024
PLE_DIM = 256
N_DIFF_HEADS = 8
DIFF_HEAD_DIM = 64
DIFF_V_DIM = 128
N_RET_HEADS = 4
RET_KEY_DIM = 128
RET_VAL_DIM = 256
D_FF = 4096
D_IN = 8192
EPS = 1e-6
LAM_INIT = 0.8 - 0.6 * math.exp(-0.3 * 0)

QK_COLS = 2 * 1024
ATT_COLS = 3 * 1024
QR_BLK, KR_BLK = 0, 1
VR_BLK, GR_BLK, GATE_A_BLK, GATE_R_BLK = 1, 2, 3, 4

ATT_T = 256
RET_C = 256
NEG = -1e30
LOG2E = math.log2(math.e)
ONES_ROWS = 16
ALIBI_PIECES = 3
VMEM_LIMIT = 56 * 1024 * 1024


def _sigmoid(x):
    return 0.5 * jnp.tanh(0.5 * x) + 0.5


def _rms(x, g):
    ms = jnp.mean(x * x, axis=-1, keepdims=True)
    return x * lax.rsqrt(ms + EPS) * g


def _in_proj_kernel(x_ref, g_ref, w_ref, qk_ref, vT_ref, rest_ref, *, tn):
    h = _rms(x_ref[...], g_ref[...]).astype(BF16)
    hd = 2 * DIFF_HEAD_DIM
    per = tn // hd
    tm = x_ref.shape[0]
    vT_ref[0, :, DIFF_V_DIM:, :] = jnp.ones((N_DIFF_HEADS, ONES_ROWS, tm), BF16)
    for n in range(D_IN // tn):
        val = jnp.dot(h, w_ref[:, n * tn:(n + 1) * tn], preferred_element_type=F32).astype(BF16)
        if (n + 1) * tn <= QK_COLS:
            for j in range(per):
                qk_ref[0, n * per + j] = val[:, j * hd:(j + 1) * hd]
        elif (n + 1) * tn <= ATT_COLS:
            for j in range(per):
                vT_ref[0, (n * tn - QK_COLS) // hd + j, 0:DIFF_V_DIM, :] = (
                    val[:, j * hd:(j + 1) * hd].T)
        else:
            rest_ref[:, n * tn - ATT_COLS:(n + 1) * tn - ATT_COLS] = val


def _in_proj(x2d, g_mix, w_in_bf16, B, S, tm=512, tn=1024):
    T = x2d.shape[0]
    hd = 2 * DIFF_HEAD_DIM
    spb = S // tm
    return pl.pallas_call(
        functools.partial(_in_proj_kernel, tn=tn),
        grid=(T // tm,),
        in_specs=[
            pl.BlockSpec((tm, D_MODEL), lambda i: (i, 0)),
            pl.BlockSpec((1, D_MODEL), lambda i: (0, 0)),
            pl.BlockSpec((D_MODEL, D_IN), lambda i: (0, 0), pipeline_mode=pl.Buffered(1)),
        ],
        out_specs=[
            pl.BlockSpec((1, QK_COLS // hd, tm, hd), lambda i: (i // spb, 0, i % spb, 0)),
            pl.BlockSpec((1, N_DIFF_HEADS, DIFF_V_DIM + ONES_ROWS, tm),
                         lambda i: (i // spb, 0, 0, i % spb)),
            pl.BlockSpec((tm, D_IN - ATT_COLS), lambda i: (i, 0)),
        ],
        out_shape=[jax.ShapeDtypeStruct((B, QK_COLS // hd, S, hd), BF16),
                   jax.ShapeDtypeStruct((B, N_DIFF_HEADS, DIFF_V_DIM + ONES_ROWS, S), BF16),
                   jax.ShapeDtypeStruct((T, D_IN - ATT_COLS), BF16)],
        compiler_params=pltpu.CompilerParams(
            dimension_semantics=("arbitrary",), vmem_limit_bytes=VMEM_LIMIT),
        name="in_proj",
    )(x2d, g_mix, w_in_bf16)


def _attn_kernel(lq1_ref, lk1_ref, lq2_ref, lk2_ref, gcol_ref, kb_ref, att_ref, vT_ref, o_ref,
                 y_scr):
    T = ATT_T
    S = att_ref.shape[2]
    nq = S // T
    hd = 2 * DIFF_HEAD_DIM

    lam = (jnp.exp(jnp.sum(lq1_ref[...] * lk1_ref[...], axis=-1, keepdims=True))
           - jnp.exp(jnp.sum(lq2_ref[...] * lk2_ref[...], axis=-1, keepdims=True))
           + LAM_INIT)

    causal = (lax.broadcasted_iota(jnp.int32, (T, 2 * T), 0)
              <= (lax.broadcasted_iota(jnp.int32, (T, 2 * T), 1) & (T - 1)))
    lane = lax.broadcasted_iota(jnp.int32, (T, hd), 1)
    ones_cols = jnp.where(lane < ALIBI_PIECES, 1.0, 0.0).astype(BF16)
    nt = (((1,), (1,)), ((), ()))

    g_col = jnp.broadcast_to(gcol_ref[...], (DIFF_V_DIM, T))

    def head(h, carry):
        def query_operand(qi):
            q = att_ref[0, h, qi * T:(qi + 1) * T, :].astype(F32)
            q = (q * (LOG2E * DIFF_HEAD_DIM ** -0.5)).astype(BF16)
            zero = jnp.zeros_like(q)
            qa = jnp.concatenate([jnp.where(lane < DIFF_HEAD_DIM, q, zero), ones_cols], axis=1)
            qb = jnp.concatenate([jnp.where(lane >= DIFF_HEAD_DIM, q, zero), ones_cols], axis=1)
            return jnp.concatenate([qa, qb], axis=0)

        def stage_a(qi):
            n = (qi + 1) * T
            k_aug = jnp.concatenate([att_ref[0, N_DIFF_HEADS + h, 0:n, :], kb_ref[h, 0:n, :]],
                                    axis=1)
            u = lax.dot_general(k_aug, query_operand(qi), nt, preferred_element_type=F32)
            tiles = [u[j * T:(j + 1) * T, :] for j in range(qi)]
            tiles.append(jnp.where(causal, u[qi * T:, :], NEG))
            m = jnp.max(tiles[-1], axis=0, keepdims=True)
            if qi > 0:
                m = jnp.maximum(m, jnp.max(u[:qi * T, :], axis=0, keepdims=True))
            return tiles, m

        def stage_b(qi, tiles, m):
            acc = jnp.zeros((DIFF_V_DIM + ONES_ROWS, 2 * T), F32)
            for j, u in enumerate(tiles):
                p = jnp.exp2(u - m).astype(BF16)
                acc = acc + jnp.dot(vT_ref[0, h, :, j * T:(j + 1) * T], p,
                                    preferred_element_type=F32)
            r = acc[0:DIFF_V_DIM] * (1.0 / acc[DIFF_V_DIM:DIFF_V_DIM + 1])
            o = r[:, :T] - lam * r[:, T:]
            ms = jnp.mean(o * o, axis=0, keepdims=True)
            y = (o * (lax.rsqrt(ms + EPS) * (1.0 - LAM_INIT)) * g_col).astype(y_scr.dtype)
            y_scr[h, qi * T:(qi + 1) * T, :] = y.T

        pend = []
        for qi in range(nq):
            pend.append((qi, stage_a(qi)))
            if len(pend) > 2:
                j, (tiles, m) = pend.pop(0)
                stage_b(j, tiles, m)
        for j, (tiles, m) in pend:
            stage_b(j, tiles, m)
        return carry

    lax.fori_loop(0, N_DIFF_HEADS, head, 0)
    for h in range(N_DIFF_HEADS):
        o_ref[:, h * DIFF_V_DIM:(h + 1) * DIFF_V_DIM] = y_scr[h]


def _alibi_key_table(S):
    slopes = 2.0 ** (-8.0 * jnp.arange(1, N_DIFF_HEADS + 1, dtype=F32) / N_DIFF_HEADS)
    rem = (LOG2E * slopes)[:, None] * jnp.arange(S, dtype=F32)[None, :]
    pieces = []
    for _ in range(ALIBI_PIECES):
        piece = lax.bitcast_convert_type(
            lax.bitcast_convert_type(rem, jnp.uint32) & jnp.uint32(0xFFFF0000), F32)
        pieces.append(piece.astype(BF16))
        rem = rem - piece
    tab = jnp.stack(pieces, axis=-1)
    return jnp.pad(tab, ((0, 0), (0, 0), (0, 2 * DIFF_HEAD_DIM - ALIBI_PIECES)))


def _diff_attn(qk, vT, lq1, lk1, lq2, lk2, g_sub):
    B, G, S, hd = qk.shape
    small = lambda: pl.BlockSpec((1, DIFF_HEAD_DIM), lambda b: (0, 0))
    return pl.pallas_call(
        _attn_kernel,
        grid=(B,),
        in_specs=[
            small(), small(), small(), small(),
            pl.BlockSpec((DIFF_V_DIM, 1), lambda b: (0, 0)),
            pl.BlockSpec((N_DIFF_HEADS, S, hd), lambda b: (0, 0, 0), pipeline_mode=pl.Buffered(1)),
            pl.BlockSpec((1, G, S, hd), lambda b: (b, 0, 0, 0)),
            pl.BlockSpec((1, N_DIFF_HEADS, DIFF_V_DIM + ONES_ROWS, S), lambda b: (b, 0, 0, 0)),
        ],
        out_specs=pl.BlockSpec((S, N_DIFF_HEADS * DIFF_V_DIM), lambda b: (b, 0)),
        out_shape=jax.ShapeDtypeStruct((B * S, N_DIFF_HEADS * DIFF_V_DIM), BF16),
        scratch_shapes=[
            pltpu.VMEM((N_DIFF_HEADS, S, DIFF_V_DIM), BF16),
        ],
        compiler_params=pltpu.CompilerParams(
            dimension_semantics=("arbitrary",), vmem_limit_bytes=VMEM_LIMIT),
        name="diff_attn",
    )(lq1, lk1, lq2, lk2, g_sub, _alibi_key_table(S), qk, vT)


def _ret_kernel(cdec_ref, dmat_ref, qdec_ref, kdec_ref, g_ref,
                q_ref, k_ref, v_ref, gr_ref, o_ref):
    C = RET_C
    S = q_ref.shape[0]
    nt = (((1,), (1,)), ((), ()))
    for h in range(N_RET_HEADS):
        kcol = slice(h * RET_KEY_DIM, (h + 1) * RET_KEY_DIM)
        vcol = slice(h * RET_VAL_DIM, (h + 1) * RET_VAL_DIM)
        cd = cdec_ref[h]
        r = jnp.zeros((RET_KEY_DIM, RET_VAL_DIM), F32)
        for i in range(S // C):
            rows = slice(i * C, (i + 1) * C)
            qi = q_ref[rows, kcol]
            ki = k_ref[rows, kcol]
            vi = v_ref[rows, vcol]
            s = lax.dot_general(qi, ki, nt, preferred_element_type=F32) * dmat_ref[h]
            if i == 0:
                o = jnp.dot(s.astype(BF16), vi, preferred_element_type=F32)
            else:
                qd = (qi.astype(F32) * qdec_ref[h][:, :RET_KEY_DIM]).astype(BF16)
                o = jnp.dot(jnp.concatenate([s.astype(BF16), qd], axis=1),
                            jnp.concatenate([vi, r.astype(BF16)], axis=0),
                            preferred_element_type=F32)
            if i + 1 < S // C:
                kd = (ki.astype(F32) * kdec_ref[h]).astype(BF16)
                r = r * cd + jnp.dot(kd.T, vi, preferred_element_type=F32)
            y = _rms(o, g_ref[h])
            gr = gr_ref[rows, vcol].astype(F32)
            o_ref[rows, vcol] = (y * (gr * _sigmoid(gr))).astype(o_ref.dtype)


def _retention(rest, g_ret_sub, B, S):
    T = B * S
    C = RET_C
    H = N_RET_HEADS
    hh = jnp.arange(H, dtype=F32)
    log_gamma = jnp.log(1.0 - 2.0 ** (-5.0 - hh))
    pos = jnp.arange(C, dtype=F32)
    rel = pos[:, None] - pos[None, :]
    kscale = RET_KEY_DIM ** -0.5
    dmat = jnp.where(rel >= 0, jnp.exp(log_gamma[:, None, None] * rel), 0.0) * kscale
    qdec = jnp.broadcast_to(jnp.exp(log_gamma[:, None] * (pos[None, :] + 1.0))[:, :, None],
                            (H, C, RET_VAL_DIM))
    kdec = jnp.broadcast_to(
        (jnp.exp(log_gamma[:, None] * (C - 1.0 - pos[None, :])) * kscale)[:, :, None],
        (H, C, RET_KEY_DIM))
    cdec = jnp.exp(log_gamma * C)
    whole = lambda shape: pl.BlockSpec(shape, lambda b: (0,) * len(shape))
    return pl.pallas_call(
        _ret_kernel,
        grid=(B,),
        in_specs=[
            pl.BlockSpec(memory_space=pltpu.SMEM),
            whole((H, C, C)), whole((H, C, RET_VAL_DIM)), whole((H, C, RET_KEY_DIM)),
            whole((H, 1, RET_VAL_DIM)),
            pl.BlockSpec((S, H * RET_KEY_DIM), lambda b: (b, QR_BLK)),
            pl.BlockSpec((S, H * RET_KEY_DIM), lambda b: (b, KR_BLK)),
            pl.BlockSpec((S, H * RET_VAL_DIM), lambda b: (b, VR_BLK)),
            pl.BlockSpec((S, H * RET_VAL_DIM), lambda b: (b, GR_BLK)),
        ],
        out_specs=pl.BlockSpec((S, H * RET_VAL_DIM), lambda b: (b, 0)),
        out_shape=jax.ShapeDtypeStruct((T, H * RET_VAL_DIM), BF16),
        compiler_params=pltpu.CompilerParams(
            dimension_semantics=("arbitrary",), vmem_limit_bytes=VMEM_LIMIT),
        name="retention",
    )(cdec, dmat, qdec, kdec, g_ret_sub.reshape(H, 1, RET_VAL_DIM), rest, rest, rest, rest)


def _merge_kernel(x_ref, ya_ref, yr_ref, ga_ref, gr_ref, wbd_ref, wbr_ref, wo_ref, o_ref,
                  *, n_sub):
    ts = x_ref.shape[0] // n_sub

    def mix(s):
        rows = slice(s * ts, (s + 1) * ts)
        a = jnp.dot(ya_ref[rows, :], wbd_ref[...], preferred_element_type=F32)
        r = jnp.dot(yr_ref[rows, :], wbr_ref[...], preferred_element_type=F32)
        return (_sigmoid(ga_ref[rows, :].astype(F32)) * a
                + _sigmoid(gr_ref[rows, :].astype(F32)) * r).astype(BF16)

    def project(s, mixed):
        rows = slice(s * ts, (s + 1) * ts)
        o_ref[rows, :] = x_ref[rows, :] + jnp.dot(mixed, wo_ref[...], preferred_element_type=F32)

    prev = None
    for s in range(n_sub):
        cur = mix(s)
        if prev is not None:
            project(s - 1, prev)
        prev = cur
    project(n_sub - 1, prev)


def _merge(x2d, ya, yr, rest, wbd, wbr, wo, tm=1024, n_sub=2):
    T = x2d.shape[0]
    row = lambda c: pl.BlockSpec((tm, D_MODEL), lambda i: (i, c))
    wspec = lambda: pl.BlockSpec((D_MODEL, D_MODEL), lambda i: (0, 0))
    return pl.pallas_call(
        functools.partial(_merge_kernel, n_sub=n_sub),
        grid=(T // tm,),
        in_specs=[row(0), row(0), row(0), row(GATE_A_BLK), row(GATE_R_BLK),
                  wspec(), wspec(), wspec()],
        out_specs=row(0),
        out_shape=jax.ShapeDtypeStruct((T, D_MODEL), F32),
        compiler_params=pltpu.CompilerParams(
            dimension_semantics=("arbitrary",), vmem_limit_bytes=VMEM_LIMIT),
        name="merge",
    )(x2d, ya, yr, rest, rest, wbd, wbr, wo)


def _mlp_kernel(x_ref, p_ref, gm_ref, w1_ref, w2_ref, gp_ref, wpg_ref, wple_ref, gf_ref, o_ref,
                *, n_sub):
    ts = x_ref.shape[0] // n_sub

    def ffn(s):
        x1 = x_ref[s * ts:(s + 1) * ts, :]
        h2 = _rms(x1, gm_ref[...]).astype(BF16)
        acc = jnp.zeros_like(x1)
        for f in range(D_FF // D_MODEL):
            sl = slice(f * D_MODEL, (f + 1) * D_MODEL)
            a = jnp.maximum(jnp.dot(h2, w1_ref[:, sl], preferred_element_type=F32), 0.0)
            acc = acc + jnp.dot((a * a).astype(BF16), w2_ref[sl, :], preferred_element_type=F32)
        return x1 + acc

    def ple_and_norm(s, x2):
        rows = slice(s * ts, (s + 1) * ts)
        hp = _rms(x2, gp_ref[...]).astype(BF16)
        gate = _sigmoid(jnp.dot(hp, wpg_ref[...], preferred_element_type=F32))
        pe = jnp.dot(p_ref[rows, :].astype(BF16), wple_ref[...], preferred_element_type=F32)
        o_ref[rows, :] = _rms(x2 + gate * pe, gf_ref[...])

    prev = None
    for s in range(n_sub):
        cur = ffn(s)
        if prev is not None:
            ple_and_norm(s - 1, prev)
        prev = cur
    ple_and_norm(n_sub - 1, prev)


def _mlp_ple(x1, p2d, g_mlp, w1, w2, g_ple, wpg, wple, g_final, tm=1024, n_sub=4):
    T = x1.shape[0]
    row = pl.BlockSpec((tm, D_MODEL), lambda i: (i, 0))
    vec = lambda: pl.BlockSpec((1, D_MODEL), lambda i: (0, 0))
    res = lambda shape: pl.BlockSpec(shape, lambda i: (0, 0), pipeline_mode=pl.Buffered(1))
    return pl.pallas_call(
        functools.partial(_mlp_kernel, n_sub=n_sub),
        grid=(T // tm,),
        in_specs=[row, pl.BlockSpec((tm, PLE_DIM), lambda i: (i, 0)), vec(),
                  res((D_MODEL, D_FF)), res((D_FF, D_MODEL)), vec(),
                  res((D_MODEL, D_MODEL)), res((PLE_DIM, D_MODEL)), vec()],
        out_specs=row,
        out_shape=jax.ShapeDtypeStruct((T, D_MODEL), F32),
        compiler_params=pltpu.CompilerParams(
            dimension_semantics=("arbitrary",), vmem_limit_bytes=VMEM_LIMIT),
        name="mlp_ple",
    )(x1, p2d, g_mlp, w1, w2, g_ple, wpg, wple, g_final)


def kernel(x, p, g_mix, w_in, lam_q1, lam_k1, lam_q2, lam_k2, g_diff_sub, g_ret_sub,
           w_branch_diff, w_branch_ret, w_out, g_mlp, w_ff1, w_ff2, g_ple, w_ple_gate,
           w_ple, g_final):
    B, S, D = x.shape
    T = B * S
    x2d = x.reshape(T, D)
    bf = lambda w: w.astype(BF16)

    qk, vT, rest = _in_proj(x2d, g_mix[0][None, :], bf(w_in[0]), B, S)
    ya = _diff_attn(qk, vT, lam_q1[0][None, :], lam_k1[0][None, :],
                    lam_q2[0][None, :], lam_k2[0][None, :], g_diff_sub[0][:, None])
    yr = _retention(rest, g_ret_sub[0], B, S)
    x1 = _merge(x2d, ya, yr, rest, bf(w_branch_diff[0]), bf(w_branch_ret[0]), bf(w_out[0]))
    out = _mlp_ple(x1, p[0].reshape(T, PLE_DIM), g_mlp[0][None, :], bf(w_ff1[0]), bf(w_ff2[0]),
                   g_ple[0][None, :], bf(w_ple_gate[0]), bf(w_ple[0]), g_final[None, :])
    return out.reshape(B, S, D)
```
